```python
import jax, jax.numpy as jnp
from jax import lax
import numpy as np

D_MODEL = 1024
BATCH = 8
SEQ = 4096
DEPTH = 2

N_MIXERS = 2
D_FF = 2816
CONV_WIDTH = 31
ATTN_GROUPS = ((128, 1), (512, 4), (2048, 16))
N_GROUPS = len(ATTN_GROUPS)
HEADS_PER_GROUP = 8
HEAD_DIM = 128
QKV_COLS = N_GROUPS * 3 * HEADS_PER_GROUP * HEAD_DIM
ATTN_OUT = HEADS_PER_GROUP * HEAD_DIM
ATTN_BLOCK = 128
NORM_EPS = 1e-6
N_CONV_LAYERS = (DEPTH + N_MIXERS - 1) // N_MIXERS
N_ATTN_LAYERS = DEPTH // N_MIXERS

kernel_name = "hybrid_conformer_conv_dilated_attn_macaron"


def rmsnorm(x, g):
    xf = x.astype(jnp.float32)
    y = xf * lax.rsqrt(jnp.mean(xf * xf, axis=-1, keepdims=True) + NORM_EPS)
    return (y * g.astype(jnp.float32)).astype(x.dtype)


def swiglu_ffn(h, w_in, w_out):
    gate, up = jnp.split(h @ w_in, 2, axis=-1)
    return (jax.nn.silu(gate) * up) @ w_out


def conformer_conv_module(h, w_pw1, b_pw1, w_dw, b_dw, norm_g, w_pw2, b_pw2):
    a, gate = jnp.split(h @ w_pw1 + b_pw1, 2, axis=-1)
    u = a * jax.nn.sigmoid(gate)
    u = lax.conv_general_dilated(
        u, w_dw[:, None, :].astype(u.dtype), window_strides=(1,),
        padding=((CONV_WIDTH - 1, 0),),
        dimension_numbers=("NWC", "WIO", "NWC"),
        feature_group_count=D_MODEL) + b_dw
    u = jax.nn.silu(rmsnorm(u, norm_g))
    return u @ w_pw2 + b_pw2


def _band_mask(n_blocks, steps):
    i = jnp.arange(ATTN_BLOCK)[:, None]
    j = jnp.arange(2 * ATTN_BLOCK)[None, :]
    diff = ATTN_BLOCK + i - j
    band = (diff >= 0) & (diff <= steps)
    key_pos = jnp.arange(n_blocks)[:, None, None] * ATTN_BLOCK - ATTN_BLOCK + j[None]
    return band[None] & (key_pos >= 0)


def dilated_window_attention(q, k, v, window, dilation):
    B, S, H, E = q.shape
    steps = window // dilation
    L = S // dilation
    nb = -(-L // ATTN_BLOCK)
    Lp = nb * ATTN_BLOCK

    def to_streams(t):
        t = t.reshape(B, L, dilation, H, E).transpose(0, 2, 1, 3, 4)
        t = jnp.pad(t, ((0, 0), (0, 0), (0, Lp - L), (0, 0), (0, 0)))
        return t.reshape(B, dilation, nb, ATTN_BLOCK, H, E)

    def with_prev(t):
        prev = jnp.concatenate([jnp.zeros_like(t[:, :, :1]), t[:, :, :-1]], axis=2)
        return jnp.concatenate([prev, t], axis=3)

    qb = to_streams(q)
    kw = with_prev(to_streams(k))
    vw = with_prev(to_streams(v))
    s = jnp.einsum("bdnqhe,bdnkhe->bdnhqk", qb, kw).astype(jnp.float32) * (HEAD_DIM ** -0.5)
    valid = _band_mask(nb, steps)[None, None, :, None]
    s = jnp.where(valid, s, -jnp.inf)
    m = jnp.max(s, axis=-1, keepdims=True)
    p = jnp.exp(s - m)
    l = jnp.sum(p, axis=-1, keepdims=True)
    o = jnp.einsum("bdnhqk,bdnkhe->bdnqhe", p / l, vw.astype(jnp.float32))
    lse = (m + jnp.log(l))[..., 0]
    o = o.reshape(B, dilation, Lp, H, E)[:, :, :L].transpose(0, 2, 1, 3, 4).reshape(B, S, H, E)
    lse = lse.transpose(0, 1, 2, 4, 3).reshape(B, dilation, Lp, H)[:, :, :L]
    lse = lse.transpose(0, 2, 1, 3).reshape(B, S, H)
    return o, lse


def dilated_attention_mixer(h, w_qkv, q_norm, k_norm, w_o):
    B, S, _ = h.shape
    qkv = (h @ w_qkv).reshape(B, S, N_GROUPS, 3, HEADS_PER_GROUP, HEAD_DIM)
    outs, lses = [], []
    for g, (window, dilation) in enumerate(ATTN_GROUPS):
        q = rmsnorm(qkv[:, :, g, 0], q_norm[g])
        k = rmsnorm(qkv[:, :, g, 1], k_norm[g])
        o, lse = dilated_window_attention(q, k, qkv[:, :, g, 2], window, dilation)
        outs.append(o)
        lses.append(lse)
    wts = jax.nn.softmax(jnp.stack(lses), axis=0)
    o = jnp.einsum("gbsh,gbshe->bshe", wts, jnp.stack(outs))
    return o.reshape(B, S, ATTN_OUT).astype(h.dtype) @ w_o


def setup_inputs(seed: int = 0) -> dict:
    key = jax.random.key(seed)
    ks = jax.random.split(key, 16)
    nrm = lambda k, shape, scale: jax.random.normal(k, shape, jnp.float32) * scale
    return {
        "x": nrm(ks[0], (BATCH, SEQ, D_MODEL), 1.0),
        "norm_g": 1.0 + nrm(ks[1], (DEPTH, 3, D_MODEL), 0.02),
        "ffn_w_in": nrm(ks[2], (DEPTH, 2, D_MODEL, 2 * D_FF), D_MODEL ** -0.5),
        "ffn_w_out": nrm(ks[3], (DEPTH, 2, D_FF, D_MODEL), D_FF ** -0.5),
        "conv_w_pw1": nrm(ks[4], (N_CONV_LAYERS, D_MODEL, 2 * D_MODEL), D_MODEL ** -0.5),
        "conv_b_pw1": nrm(ks[5], (N_CONV_LAYERS, 2 * D_MODEL), 0.01),
        "conv_w_dw": nrm(ks[6], (N_CONV_LAYERS, CONV_WIDTH, D_MODEL), CONV_WIDTH ** -0.5),
        "conv_b_dw": nrm(ks[7], (N_CONV_LAYERS, D_MODEL), 0.01),
        "conv_norm_g": 1.0 + nrm(ks[8], (N_CONV_LAYERS, D_MODEL), 0.02),
        "conv_w_pw2": nrm(ks[9], (N_CONV_LAYERS, D_MODEL, D_MODEL), D_MODEL ** -0.5),
        "conv_b_pw2": nrm(ks[10], (N_CONV_LAYERS, D_MODEL), 0.01),
        "attn_w_qkv": nrm(ks[11], (N_ATTN_LAYERS, D_MODEL, QKV_COLS), D_MODEL ** -0.5),
        "attn_q_norm": 1.0 + nrm(ks[12], (N_ATTN_LAYERS, N_GROUPS, HEAD_DIM), 0.02),
        "attn_k_norm": 1.0 + nrm(ks[13], (N_ATTN_LAYERS, N_GROUPS, HEAD_DIM), 0.02),
        "attn_w_o": nrm(ks[14], (N_ATTN_LAYERS, ATTN_OUT, D_MODEL), ATTN_OUT ** -0.5),
    }


def reference(x, norm_g, ffn_w_in, ffn_w_out, conv_w_pw1, conv_b_pw1, conv_w_dw, conv_b_dw,
              conv_norm_g, conv_w_pw2, conv_b_pw2, attn_w_qkv, attn_q_norm, attn_k_norm, attn_w_o):
    for layer in range(DEPTH):
        x = x + 0.5 * swiglu_ffn(rmsnorm(x, norm_g[layer, 0]), ffn_w_in[layer, 0], ffn_w_out[layer, 0])
        h = rmsnorm(x, norm_g[layer, 1])
        idx = layer // N_MIXERS
        if layer % N_MIXERS == 0:
            mix = conformer_conv_module(h, conv_w_pw1[idx], conv_b_pw1[idx], conv_w_dw[idx], conv_b_dw[idx],
                                        conv_norm_g[idx], conv_w_pw2[idx], conv_b_pw2[idx])
        else:
            mix = dilated_attention_mixer(h, attn_w_qkv[idx], attn_q_norm[idx], attn_k_norm[idx], attn_w_o[idx])
        x = x + mix
        x = x + 0.5 * swiglu_ffn(rmsnorm(x, norm_g[layer, 2]), ffn_w_in[layer, 1], ffn_w_out[layer, 1])
    return x
```

```python
import functools

import jax
import jax.numpy as jnp
from jax import lax
from jax.experimental import pallas as pl
from jax.experimental.pallas import tpu as pltpu

D_MODEL = 1024
D_FF = 2816
CONV_WIDTH = 31
ATTN_GROUPS = ((128, 1), (512, 4), (2048, 16))
HEADS = 8
HEAD_DIM = 128
ATTN_BLOCK = 128
NORM_EPS = 1e-6
MASKED = -1e30
LANES = 128
CONV_HALO = 32

F32 = jnp.float32
BF16 = jnp.bfloat16

FFN_TOKENS = 512
FFN_CHUNK = 256
CONV_TOKENS = 512
CONV_ROWS = 64
QKV_TOKENS = 1024
MERGE_TOKENS = 512
ATTN_TILING = ((4, 8), (2, 8), (1, 4))
VMEM_LIMIT = 56 * 1024 * 1024


def _resident(shape):
    return pl.BlockSpec(shape, lambda *_: (0,) * len(shape), pipeline_mode=pl.Buffered(1))


def _params(*semantics):
    return pltpu.CompilerParams(dimension_semantics=semantics, vmem_limit_bytes=VMEM_LIMIT)


def _rmsnorm(x, g):
    return x * lax.rsqrt(jnp.mean(x * x, axis=-1, keepdims=True) + NORM_EPS) * g


def _sigmoid(x):
    return 1.0 / (1.0 + jnp.exp(-x))


def _ffn_kernel(x_ref, g_ref, win_ref, wout_ref, o_ref, h_ref):
    x = x_ref[...]
    h_ref[...] = _rmsnorm(x, g_ref[...]).astype(BF16)
    acc = jnp.zeros(x.shape, F32)
    for c in range(D_FF // FFN_CHUNK):
        lo = c * FFN_CHUNK
        h = h_ref[...]
        gate = jnp.dot(h, win_ref[:, lo:lo + FFN_CHUNK], preferred_element_type=F32)
        up = jnp.dot(h, win_ref[:, D_FF + lo:D_FF + lo + FFN_CHUNK], preferred_element_type=F32)
        a = (gate * _sigmoid(gate) * up).astype(BF16)
        acc = acc + jnp.dot(a, wout_ref[lo:lo + FFN_CHUNK, :], preferred_element_type=F32)
    o_ref[...] = x + 0.5 * acc


def _ffn(x, g, w_in, w_out):
    tokens = x.shape[0]
    return pl.pallas_call(
        _ffn_kernel,
        grid=(tokens // FFN_TOKENS,),
        in_specs=[
            pl.BlockSpec((FFN_TOKENS, D_MODEL), lambda i: (i, 0)),
            _resident((1, D_MODEL)),
            _resident((D_MODEL, 2 * D_FF)),
            _resident((D_FF, D_MODEL)),
        ],
        out_specs=pl.BlockSpec((FFN_TOKENS, D_MODEL), lambda i: (i, 0)),
        out_shape=jax.ShapeDtypeStruct(x.shape, F32),
        scratch_shapes=[pltpu.VMEM((FFN_TOKENS, D_MODEL), BF16)],
        compiler_params=_params("parallel"),
        name="ffn",
    )(x, g.reshape(1, D_MODEL), w_in.astype(BF16), w_out.astype(BF16))


def _conv_kernel(x_ref, g_ref, w1_ref, b1_ref, wdw_ref, bdw_ref, gc_ref, w2_ref, b2_ref, o_ref,
                 u_ref, y_ref):
    ts = x_ref.shape[0]

    @pl.when(pl.program_id(1) == 0)
    def _():
        u_ref[0:CONV_HALO, :] = jnp.zeros((CONV_HALO, D_MODEL), F32)

    x = x_ref[...]
    h = _rmsnorm(x, g_ref[...]).astype(BF16)
    ag = jnp.dot(h, w1_ref[...], preferred_element_type=F32) + b1_ref[...]
    u_ref[CONV_HALO:CONV_HALO + ts, :] = ag[:, :D_MODEL] * _sigmoid(ag[:, D_MODEL:])

    first = CONV_HALO - (CONV_WIDTH - 1)
    for cb in range(D_MODEL // LANES):
        cols = pl.ds(cb * LANES, LANES)
        w = wdw_ref[:, cols]
        b = bdw_ref[:, cols]
        for rc in range(ts // CONV_ROWS):
            acc = jnp.broadcast_to(b, (CONV_ROWS, LANES))
            for k in range(CONV_WIDTH):
                acc = acc + w[k:k + 1, :] * u_ref[pl.ds(first + k + rc * CONV_ROWS, CONV_ROWS), cols]
            y_ref[pl.ds(rc * CONV_ROWS, CONV_ROWS), cols] = acc

    u_ref[0:CONV_HALO, :] = u_ref[ts:ts + CONV_HALO, :]

    y = _rmsnorm(y_ref[...], gc_ref[...])
    y = (y * _sigmoid(y)).astype(BF16)
    o_ref[...] = x + jnp.dot(y, w2_ref[...], preferred_element_type=F32) + b2_ref[...]


def _conv(x, g, w_pw1, b_pw1, w_dw, b_dw, g_conv, w_pw2, b_pw2, batch, seq):
    ts = CONV_TOKENS
    x3 = x.reshape(batch, seq, D_MODEL)
    tile = pl.BlockSpec((None, ts, D_MODEL), lambda b, s: (b, s, 0))
    out = pl.pallas_call(
        _conv_kernel,
        grid=(batch, seq // ts),
        in_specs=[
            tile,
            _resident((1, D_MODEL)),
            _resident((D_MODEL, 2 * D_MODEL)),
            _resident((1, 2 * D_MODEL)),
            _resident((CONV_WIDTH, D_MODEL)),
            _resident((1, D_MODEL)),
            _resident((1, D_MODEL)),
            _resident((D_MODEL, D_MODEL)),
            _resident((1, D_MODEL)),
        ],
        out_specs=tile,
        out_shape=jax.ShapeDtypeStruct(x3.shape, F32),
        scratch_shapes=[pltpu.VMEM((CONV_HALO + ts, D_MODEL), F32), pltpu.VMEM((ts, D_MODEL), F32)],
        compiler_params=_params("parallel", "arbitrary"),
        name="conv",
    )(x3, g.reshape(1, D_MODEL), w_pw1.astype(BF16), b_pw1.reshape(1, -1), w_dw, b_dw.reshape(1, -1),
      g_conv.reshape(1, -1), w_pw2.astype(BF16), b_pw2.reshape(1, -1))
    return out.reshape(x.shape)


def _qkv_kernel(x_ref, g_ref, w_ref, gain_ref, o_ref, h_ref, *stage, dilation):
    j = pl.program_id(1)

    @pl.when(j == 0)
    def _():
        h_ref[...] = _rmsnorm(x_ref[...], g_ref[...]).astype(BF16)

    acc = jnp.dot(h_ref[...], w_ref[...], preferred_element_type=F32)
    rows = acc.shape[0] // dilation

    def emit(head, val):
        if dilation == 1:
            o_ref[head, 0] = val.astype(BF16)
        else:
            stage[0][head] = val
            for r in range(dilation):
                o_ref[head, r] = stage[0][head, pl.ds(r, rows, stride=dilation), :].astype(BF16)

    @pl.when(j < 2)
    def _():
        gain = gain_ref[...]
        for head in range(HEADS):
            cols = slice(head * HEAD_DIM, (head + 1) * HEAD_DIM)
            blk = acc[:, cols]
            emit(head, _rmsnorm(blk, gain[:, cols]))

    @pl.when(j == 2)
    def _():
        for head in range(HEADS):
            emit(head, acc[:, head * HEAD_DIM:(head + 1) * HEAD_DIM])


def _qkv(x, g, w, gains, dilation, batch, seq):
    tm = QKV_TOKENS
    tiles = seq // tm
    rows = tm // dilation
    scratch = [pltpu.VMEM((tm, D_MODEL), BF16)]
    if dilation > 1:
        scratch.append(pltpu.VMEM((HEADS, tm, HEAD_DIM), F32))
    return pl.pallas_call(
        functools.partial(_qkv_kernel, dilation=dilation),
        grid=(batch * tiles, 3),
        in_specs=[
            pl.BlockSpec((tm, D_MODEL), lambda i, j: (i, 0)),
            _resident((1, D_MODEL)),
            pl.BlockSpec((D_MODEL, D_MODEL), lambda i, j: (0, j)),
            pl.BlockSpec((None, 1, D_MODEL), lambda i, j: (j, 0, 0)),
        ],
        out_specs=pl.BlockSpec((None, None, HEADS, dilation, rows, HEAD_DIM),
                               lambda i, j: (i // tiles, j, 0, 0, i % tiles, 0)),
        out_shape=jax.ShapeDtypeStruct((batch, 3, HEADS, dilation, seq // dilation, HEAD_DIM), BF16),
        scratch_shapes=scratch,
        compiler_params=_params("parallel", "arbitrary"),
        name=f"qkv_d{dilation}",
    )(x, g.reshape(1, D_MODEL), w, gains)


def _attn_kernel(q_ref, kc_ref, vc_ref, kp_ref, vp_ref, o_ref, lse_ref, *, dilation, blocks, heads):
    qb = ATTN_BLOCK
    row = lax.broadcasted_iota(jnp.int32, (qb, qb), 0)
    col = lax.broadcasted_iota(jnp.int32, (qb, qb), 1)
    bias_cur = jnp.where(col <= row, 0.0, MASKED).astype(F32)
    bias_prev = jnp.where(col >= row, 0.0, MASKED).astype(F32)
    first = pl.program_id(1) == 0
    bias_first = jnp.where(first, MASKED, bias_prev)
    lane = lax.broadcasted_iota(jnp.int32, (qb, LANES), 1)
    ones = jnp.ones((qb, HEAD_DIM), BF16)
    nt = (((1,), (1,)), ((), ()))

    def stream(r, carry):
        for b in range(blocks):
            cur = pl.ds(b * qb, qb)
            if dilation == 1:
                out_rows = cur
            else:
                out_rows = pl.ds(r + b * qb * dilation, qb, stride=dilation)
            lse_all = jnp.zeros((qb, LANES), F32)
            for h in range(heads):
                q = q_ref[h, r, cur, :]
                kc = kc_ref[h, r, cur, :]
                vc = vc_ref[h, r, cur, :]
                if b == 0:
                    kp, vp, bp = kp_ref[h, r], vp_ref[h, r], bias_first
                else:
                    prev = pl.ds((b - 1) * qb, qb)
                    kp, vp, bp = kc_ref[h, r, prev, :], vc_ref[h, r, prev, :], bias_prev
                s_c = lax.dot_general(q, kc, nt, preferred_element_type=F32) + bias_cur
                s_p = lax.dot_general(q, kp, nt, preferred_element_type=F32) + bp
                m = jnp.maximum(jnp.max(s_c, axis=-1, keepdims=True), jnp.max(s_p, axis=-1, keepdims=True))
                p_c = jnp.exp(s_c - m).astype(BF16)
                p_p = jnp.exp(s_p - m).astype(BF16)
                oa = (jnp.dot(p_c, jnp.concatenate([vc, ones], axis=1), preferred_element_type=F32)
                      + jnp.dot(p_p, jnp.concatenate([vp, ones], axis=1), preferred_element_type=F32))
                den = oa[:, HEAD_DIM:]
                o_ref[h, out_rows, :] = oa[:, :HEAD_DIM] / den
                lse_all = jnp.where(lane == h, m + jnp.log(den), lse_all)
            lse_ref[out_rows, :] = lse_all
        return carry

    if dilation == 1:
        stream(0, 0)
    else:
        lax.fori_loop(0, dilation, stream, 0)


def _attention(qkv, dilation, blocks, heads):
    batch, _, _, _, length, _ = qkv.shape
    rb = blocks * ATTN_BLOCK
    seq = length * dilation
    steps = length // rb

    def cur(which):
        return pl.BlockSpec((None, None, heads, dilation, rb, HEAD_DIM),
                            lambda b, n, hs: (b, which, hs, 0, n, 0))

    def prev(which):
        return pl.BlockSpec((None, None, heads, dilation, ATTN_BLOCK, HEAD_DIM),
                            lambda b, n, hs: (b, which, hs, 0, jnp.maximum(n * blocks - 1, 0), 0))

    return pl.pallas_call(
        functools.partial(_attn_kernel, dilation=dilation, blocks=blocks, heads=heads),
        grid=(batch, steps, HEADS // heads),
        in_specs=[cur(0), cur(1), cur(2), prev(1), prev(2)],
        out_specs=[
            pl.BlockSpec((None, heads, rb * dilation, HEAD_DIM), lambda b, n, hs: (b, hs, n, 0)),
            pl.BlockSpec((None, None, rb * dilation, LANES), lambda b, n, hs: (b, hs, n, 0)),
        ],
        out_shape=[
            jax.ShapeDtypeStruct((batch, HEADS, seq, HEAD_DIM), F32),
            jax.ShapeDtypeStruct((batch, HEADS // heads, seq, LANES), F32),
        ],
        compiler_params=_params("parallel", "parallel", "parallel"),
        name=f"attn_d{dilation}",
    )(qkv, qkv, qkv, qkv, qkv)


def _merge_kernel(x_ref, o0_ref, o1_ref, o2_ref, l0_ref, l1_ref, l2_ref, wo_ref, out_ref):
    o_refs = (o0_ref, o1_ref, o2_ref)
    l_refs = (l0_ref, l1_ref, l2_ref)
    merged = []
    for h in range(HEADS):
        lses = []
        for l_ref, (_, per_step) in zip(l_refs, ATTN_TILING):
            lses.append(l_ref[h // per_step][:, h % per_step:h % per_step + 1])
        top = jnp.maximum(jnp.maximum(lses[0], lses[1]), lses[2])
        e = [jnp.exp(l - top) for l in lses]
        mix = e[0] * o_refs[0][h] + e[1] * o_refs[1][h] + e[2] * o_refs[2][h]
        merged.append((mix / (e[0] + e[1] + e[2])).astype(BF16))
    merged = jnp.concatenate(merged, axis=1)
    out_ref[...] = x_ref[...] + jnp.dot(merged, wo_ref[...], preferred_element_type=F32)


def _merge(x, outs, lses, w_o, batch, seq):
    tm = MERGE_TOKENS
    tiles = seq // tm
    o_spec = pl.BlockSpec((None, HEADS, tm, HEAD_DIM), lambda i: (i // tiles, 0, i % tiles, 0))
    l_specs = [pl.BlockSpec((None, l.shape[1], tm, LANES), lambda i: (i // tiles, 0, i % tiles, 0)) for l in lses]
    tile = pl.BlockSpec((tm, D_MODEL), lambda i: (i, 0))
    return pl.pallas_call(
        _merge_kernel,
        grid=(batch * tiles,),
        in_specs=[tile, o_spec, o_spec, o_spec, *l_specs, _resident((D_MODEL, D_MODEL))],
        out_specs=tile,
        out_shape=jax.ShapeDtypeStruct(x.shape, F32),
        compiler_params=_params("parallel"),
        name="attn_merge",
    )(x, *outs, *lses, w_o.astype(BF16))


def _attention_mixer(x, g, w_qkv, q_norm, k_norm, w_o, batch, seq):
    cols = HEADS * HEAD_DIM
    outs, lses = [], []
    for grp, ((_, dilation), (blocks, heads)) in enumerate(zip(ATTN_GROUPS, ATTN_TILING)):
        w = w_qkv[:, grp * 3 * cols:(grp + 1) * 3 * cols].astype(BF16)
        gains = jnp.stack([jnp.tile(q_norm[grp] * HEAD_DIM ** -0.5, HEADS), jnp.tile(k_norm[grp], HEADS),
                           jnp.ones((cols,), F32)]).reshape(3, 1, cols)
        qkv = _qkv(x, g, w, gains, dilation, batch, seq)
        o, lse = _attention(qkv, dilation, blocks, heads)
        outs.append(o)
        lses.append(lse)
    return _merge(x, outs, lses, w_o, batch, seq)


def kernel(x, norm_g, ffn_w_in, ffn_w_out, conv_w_pw1, conv_b_pw1, conv_w_dw, conv_b_dw, conv_norm_g,
           conv_w_pw2, conv_b_pw2, attn_w_qkv, attn_q_norm, attn_k_norm, attn_w_o):
    batch, seq, _ = x.shape
    xt = x.reshape(batch * seq, D_MODEL)
    for layer in range(norm_g.shape[0]):
        xt = _ffn(xt, norm_g[layer, 0], ffn_w_in[layer, 0], ffn_w_out[layer, 0])
        idx = layer // 2
        if layer % 2 == 0:
            xt = _conv(xt, norm_g[layer, 1], conv_w_pw1[idx], conv_b_pw1[idx], conv_w_dw[idx], conv_b_dw[idx],
                       conv_norm_g[idx], conv_w_pw2[idx], conv_b_pw2[idx], batch, seq)
        else:
            xt = _attention_mixer(xt, norm_g[layer, 1], attn_w_qkv[idx], attn_q_norm[idx], attn_k_norm[idx],
                                  attn_w_o[idx], batch, seq)
        xt = _ffn(xt, norm_g[layer, 2], ffn_w_in[layer, 1], ffn_w_out[layer, 1])
    return xt.reshape(batch, seq, D_MODEL)
```

```python
import functools

import jax
import jax.numpy as jnp
from jax import lax
from jax.experimental import pallas as pl
from jax.experimental.pallas import tpu as pltpu

D_MODEL = 1024
D_FF = 2816
CONV_WIDTH = 31
ATTN_GROUPS = ((128, 1), (512, 4), (2048, 16))
HEADS = 8
HEAD_DIM = 128
ATTN_BLOCK = 128
NORM_EPS = 1e-6
MASKED = -1e30
LANES = 128
CONV_HALO = 32

F32 = jnp.float32
BF16 = jnp.bfloat16

FFN_TOKENS = 512
FFN_CHUNK = 256
CONV_TOKENS = 512
CONV_ROWS = 64
QKV_TOKENS = 1024
MERGE_TOKENS = 512
ATTN_TILING = ((4, 8, 1), (2, 8, 1), (1, 4, 4))
VMEM_LIMIT = 56 * 1024 * 1024


def _resident(shape):
    return pl.BlockSpec(shape, lambda *_: (0,) * len(shape), pipeline_mode=pl.Buffered(1))


def _params(*semantics):
    return pltpu.CompilerParams(dimension_semantics=semantics, vmem_limit_bytes=VMEM_LIMIT)


def _rmsnorm(x, g):
    return x * lax.rsqrt(jnp.mean(x * x, axis=-1, keepdims=True) + NORM_EPS) * g


def _sigmoid(x):
    return 1.0 / (1.0 + jnp.exp(-x))


def _ffn_kernel(x_ref, g_ref, win_ref, wout_ref, o_ref, h_ref):
    x = x_ref[...]
    h_ref[...] = _rmsnorm(x, g_ref[...]).astype(BF16)
    acc = jnp.zeros(x.shape, F32)
    for c in range(D_FF // FFN_CHUNK):
        lo = c * FFN_CHUNK
        h = h_ref[...]
        gate = jnp.dot(h, win_ref[:, lo:lo + FFN_CHUNK], preferred_element_type=F32)
        up = jnp.dot(h, win_ref[:, D_FF + lo:D_FF + lo + FFN_CHUNK], preferred_element_type=F32)
        a = (gate * _sigmoid(gate) * up).astype(BF16)
        acc = acc + jnp.dot(a, wout_ref[lo:lo + FFN_CHUNK, :], preferred_element_type=F32)
    o_ref[...] = x + 0.5 * acc


def _ffn(x, g, w_in, w_out):
    tokens = x.shape[0]
    return pl.pallas_call(
        _ffn_kernel,
        grid=(tokens // FFN_TOKENS,),
        in_specs=[
            pl.BlockSpec((FFN_TOKENS, D_MODEL), lambda i: (i, 0)),
            _resident((1, D_MODEL)),
            _resident((D_MODEL, 2 * D_FF)),
            _resident((D_FF, D_MODEL)),
        ],
        out_specs=pl.BlockSpec((FFN_TOKENS, D_MODEL), lambda i: (i, 0)),
        out_shape=jax.ShapeDtypeStruct(x.shape, F32),
        scratch_shapes=[pltpu.VMEM((FFN_TOKENS, D_MODEL), BF16)],
        compiler_params=_params("parallel"),
        name="ffn",
    )(x, g.reshape(1, D_MODEL), w_in.astype(BF16), w_out.astype(BF16))


def _conv_kernel(x_ref, g_ref, w1_ref, b1_ref, wdw_ref, bdw_ref, gc_ref, w2_ref, b2_ref, o_ref,
                 u_ref, y_ref):
    ts = x_ref.shape[0]
    slabs = D_MODEL // LANES

    @pl.when(pl.program_id(1) == 0)
    def _():
        u_ref[:, 0:CONV_HALO, :] = jnp.zeros((slabs, CONV_HALO, LANES), F32)

    x = x_ref[...]
    h = _rmsnorm(x, g_ref[...]).astype(BF16)
    ag = jnp.dot(h, w1_ref[...], preferred_element_type=F32) + b1_ref[...]
    u = ag[:, :D_MODEL] * _sigmoid(ag[:, D_MODEL:])
    for cb in range(slabs):
        u_ref[cb, CONV_HALO:CONV_HALO + ts, :] = u[:, cb * LANES:(cb + 1) * LANES]

    first = CONV_HALO - (CONV_WIDTH - 1)
    for cb in range(slabs):
        cols = pl.ds(cb * LANES, LANES)
        w = wdw_ref[:, cols]
        b = bdw_ref[:, cols]
        for parity in range(2):
            for rc in range(ts // (2 * CONV_ROWS)):
                base = parity + 2 * rc * CONV_ROWS
                acc = jnp.broadcast_to(b, (CONV_ROWS, LANES))
                for k in range(CONV_WIDTH):
                    acc = acc + w[k:k + 1, :] * u_ref[cb, pl.ds(first + k + base, CONV_ROWS, stride=2), :]
                y_ref[cb, pl.ds(base, CONV_ROWS, stride=2), :] = acc

    for cb in range(slabs):
        u_ref[cb, 0:CONV_HALO, :] = u_ref[cb, ts:ts + CONV_HALO, :]

    y = _rmsnorm(jnp.concatenate([y_ref[cb] for cb in range(slabs)], axis=1), gc_ref[...])
    y = (y * _sigmoid(y)).astype(BF16)
    o_ref[...] = x + jnp.dot(y, w2_ref[...], preferred_element_type=F32) + b2_ref[...]


def _conv(x, g, w_pw1, b_pw1, w_dw, b_dw, g_conv, w_pw2, b_pw2, batch, seq):
    ts = CONV_TOKENS
    x3 = x.reshape(batch, seq, D_MODEL)
    tile = pl.BlockSpec((None, ts, D_MODEL), lambda b, s: (b, s, 0))
    out = pl.pallas_call(
        _conv_kernel,
        grid=(batch, seq // ts),
        in_specs=[
            tile,
            _resident((1, D_MODEL)),
            _resident((D_MODEL, 2 * D_MODEL)),
            _resident((1, 2 * D_MODEL)),
            _resident((CONV_WIDTH, D_MODEL)),
            _resident((1, D_MODEL)),
            _resident((1, D_MODEL)),
            _resident((D_MODEL, D_MODEL)),
            _resident((1, D_MODEL)),
        ],
        out_specs=tile,
        out_shape=jax.ShapeDtypeStruct(x3.shape, F32),
        scratch_shapes=[pltpu.VMEM((D_MODEL // LANES, CONV_HALO + ts, LANES), F32),
                        pltpu.VMEM((D_MODEL // LANES, ts, LANES), F32)],
        compiler_params=_params("parallel", "arbitrary"),
        name="conv",
    )(x3, g.reshape(1, D_MODEL), w_pw1.astype(BF16), b_pw1.reshape(1, -1), w_dw, b_dw.reshape(1, -1),
      g_conv.reshape(1, -1), w_pw2.astype(BF16), b_pw2.reshape(1, -1))
    return out.reshape(x.shape)


def _qkv_kernel(x_ref, g_ref, w_ref, gain_ref, o_ref, h_ref, *stage, dilation):
    tm = x_ref.shape[0]
    rows = tm // dilation
    h = _rmsnorm(x_ref[...], g_ref[...])
    if dilation == 1:
        h_ref[...] = h.astype(BF16)
    else:
        slab = stage[0]
        for cb in range(D_MODEL // LANES):
            slab[cb] = h[:, cb * LANES:(cb + 1) * LANES]
        for cb in range(D_MODEL // LANES):
            for r in range(dilation):
                h_ref[r * rows:(r + 1) * rows, cb * LANES:(cb + 1) * LANES] = (
                    slab[cb, pl.ds(r, rows, stride=dilation), :].astype(BF16))

    pair = 2 * HEAD_DIM
    for j in range(3):
        for c in range(D_MODEL // pair):
            lo = j * D_MODEL + c * pair
            acc = jnp.dot(h_ref[...], w_ref[:, lo:lo + pair], preferred_element_type=F32)
            for sub in range(2):
                head = 2 * c + sub
                blk = acc[:, sub * HEAD_DIM:(sub + 1) * HEAD_DIM]
                if j < 2:
                    blk = _rmsnorm(blk, gain_ref[j:j + 1, head * HEAD_DIM:(head + 1) * HEAD_DIM])
                val = blk.astype(BF16)
                for r in range(dilation):
                    o_ref[j, head, r] = val[r * rows:(r + 1) * rows]


def _qkv(x, g, w, gains, dilation, batch, seq):
    tm = QKV_TOKENS
    tiles = seq // tm
    rows = tm // dilation
    scratch = [pltpu.VMEM((tm, D_MODEL), BF16)]
    if dilation > 1:
        scratch.append(pltpu.VMEM((D_MODEL // LANES, tm, LANES), F32))
    return pl.pallas_call(
        functools.partial(_qkv_kernel, dilation=dilation),
        grid=(batch * tiles,),
        in_specs=[
            pl.BlockSpec((tm, D_MODEL), lambda i: (i, 0)),
            _resident((1, D_MODEL)),
            _resident((D_MODEL, 3 * D_MODEL)),
            _resident((2, D_MODEL)),
        ],
        out_specs=pl.BlockSpec((None, 3, HEADS, dilation, rows, HEAD_DIM),
                               lambda i: (i // tiles, 0, 0, 0, i % tiles, 0)),
        out_shape=jax.ShapeDtypeStruct((batch, 3, HEADS, dilation, seq // dilation, HEAD_DIM), BF16),
        scratch_shapes=scratch,
        compiler_params=_params("parallel"),
        name=f"qkv_d{dilation}",
    )(x, g.reshape(1, D_MODEL), w, gains)


def _attn_kernel(q_ref, kc_ref, vc_ref, kp_ref, vp_ref, o_ref, lse_ref, *, dilation, blocks, heads, unroll):
    qb = ATTN_BLOCK
    row = lax.broadcasted_iota(jnp.int32, (qb, 2 * qb), 0)
    col = lax.broadcasted_iota(jnp.int32, (qb, 2 * qb), 1)
    bias = jnp.where((col >= row) & (col <= row + qb), 0.0, MASKED).astype(F32)
    bias_first = jnp.where((pl.program_id(1) == 0) & (col < qb), MASKED, bias)
    lane = lax.broadcasted_iota(jnp.int32, (qb, LANES), 1)
    ones = jnp.ones((2 * qb, HEAD_DIM), BF16)
    nt = (((1,), (1,)), ((), ()))

    def stream(r):
        for b in range(blocks):
            cur = pl.ds(b * qb, qb)
            if dilation == 1:
                out_rows = cur
            else:
                out_rows = pl.ds(r + b * qb * dilation, qb, stride=dilation)
            lse_all = jnp.zeros((qb, LANES), F32)
            for h in range(heads):
                q = q_ref[h, r, cur, :]
                if b == 0:
                    k2 = jnp.concatenate([kp_ref[h, r], kc_ref[h, r, cur, :]], axis=0)
                    v2 = jnp.concatenate([vp_ref[h, r], vc_ref[h, r, cur, :]], axis=0)
                    bb = bias_first
                else:
                    both = pl.ds((b - 1) * qb, 2 * qb)
                    k2, v2, bb = kc_ref[h, r, both, :], vc_ref[h, r, both, :], bias
                s = lax.dot_general(q, k2, nt, preferred_element_type=F32) + bb
                m = jnp.max(s, axis=-1, keepdims=True)
                p = jnp.exp(s - m).astype(BF16)
                oa = jnp.dot(p, jnp.concatenate([v2, ones], axis=1), preferred_element_type=F32)
                den = oa[:, HEAD_DIM:]
                o_ref[h, out_rows, :] = oa[:, :HEAD_DIM] / den
                lse_all = jnp.where(lane == h, m + jnp.log(den), lse_all)
            lse_ref[out_rows, :] = lse_all

    def body(it, carry):
        for sub in range(unroll):
            stream(it * unroll + sub)
        return carry

    if dilation == unroll:
        body(0, 0)
    else:
        lax.fori_loop(0, dilation // unroll, body, 0)


def _attention(qkv, dilation, blocks, heads, unroll):
    batch, _, _, _, length, _ = qkv.shape
    rb = blocks * ATTN_BLOCK
    seq = length * dilation
    steps = length // rb

    def cur(which):
        return pl.BlockSpec((None, None, heads, dilation, rb, HEAD_DIM),
                            lambda b, n, hs: (b, which, hs, 0, n, 0))

    def prev(which):
        return pl.BlockSpec((None, None, heads, dilation, ATTN_BLOCK, HEAD_DIM),
                            lambda b, n, hs: (b, which, hs, 0, jnp.maximum(n * blocks - 1, 0), 0))

    return pl.pallas_call(
        functools.partial(_attn_kernel, dilation=dilation, blocks=blocks, heads=heads, unroll=unroll),
        grid=(batch, steps, HEADS // heads),
        in_specs=[cur(0), cur(1), cur(2), prev(1), prev(2)],
        out_specs=[
            pl.BlockSpec((None, heads, rb * dilation, HEAD_DIM), lambda b, n, hs: (b, hs, n, 0)),
            pl.BlockSpec((None, None, rb * dilation, LANES), lambda b, n, hs: (b, hs, n, 0)),
        ],
        out_shape=[
            jax.ShapeDtypeStruct((batch, HEADS, seq, HEAD_DIM), F32),
            jax.ShapeDtypeStruct((batch, HEADS // heads, seq, LANES), F32),
        ],
        compiler_params=_params("parallel", "parallel", "parallel"),
        name=f"attn_d{dilation}",
    )(qkv, qkv, qkv, qkv, qkv)


def _merge_kernel(x_ref, o0_ref, o1_ref, o2_ref, l0_ref, l1_ref, l2_ref, wo_ref, out_ref):
    o_refs = (o0_ref, o1_ref, o2_ref)
    l_refs = (l0_ref, l1_ref, l2_ref)
    merged = []
    for h in range(HEADS):
        lses = []
        for l_ref, (_, per_step, _) in zip(l_refs, ATTN_TILING):
            lses.append(l_ref[h // per_step][:, h % per_step:h % per_step + 1])
        top = jnp.maximum(jnp.maximum(lses[0], lses[1]), lses[2])
        e = [jnp.exp(l - top) for l in lses]
        mix = e[0] * o_refs[0][h] + e[1] * o_refs[1][h] + e[2] * o_refs[2][h]
        merged.append((mix / (e[0] + e[1] + e[2])).astype(BF16))
    merged = jnp.concatenate(merged, axis=1)
    out_ref[...] = x_ref[...] + jnp.dot(merged, wo_ref[...], preferred_element_type=F32)


def _merge(x, outs, lses, w_o, batch, seq):
    tm = MERGE_TOKENS
    tiles = seq // tm
    o_spec = pl.BlockSpec((None, HEADS, tm, HEAD_DIM), lambda i: (i // tiles, 0, i % tiles, 0))
    l_specs = [pl.BlockSpec((None, l.shape[1], tm, LANES), lambda i: (i // tiles, 0, i % tiles, 0)) for l in lses]
    tile = pl.BlockSpec((tm, D_MODEL), lambda i: (i, 0))
    return pl.pallas_call(
        _merge_kernel,
        grid=(batch * tiles,),
        in_specs=[tile, o_spec, o_spec, o_spec, *l_specs, _resident((D_MODEL, D_MODEL))],
        out_specs=tile,
        out_shape=jax.ShapeDtypeStruct(x.shape, F32),
        compiler_params=_params("parallel"),
        name="attn_merge",
    )(x, *outs, *lses, w_o.astype(BF16))


def _attention_mixer(x, g, w_qkv, q_norm, k_norm, w_o, batch, seq):
    cols = HEADS * HEAD_DIM
    outs, lses = [], []
    for grp, ((_, dilation), (blocks, heads, unroll)) in enumerate(zip(ATTN_GROUPS, ATTN_TILING)):
        w = w_qkv[:, grp * 3 * cols:(grp + 1) * 3 * cols].astype(BF16)
        gains = jnp.stack([jnp.tile(q_norm[grp] * HEAD_DIM ** -0.5, HEADS), jnp.tile(k_norm[grp], HEADS)])
        qkv = _qkv(x, g, w, gains, dilation, batch, seq)
        o, lse = _attention(qkv, dilation, blocks, heads, unroll)
        outs.append(o)
        lses.append(lse)
    return _merge(x, outs, lses, w_o, batch, seq)


def kernel(x, norm_g, ffn_w_in, ffn_w_out, conv_w_pw1, conv_b_pw1, conv_w_dw, conv_b_dw, conv_norm_g,
           conv_w_pw2, conv_b_pw2, attn_w_qkv, attn_q_norm, attn_k_norm, attn_w_o):
    batch, seq, _ = x.shape
    xt = x.reshape(batch * seq, D_MODEL)
    for layer in range(norm_g.shape[0]):
        xt = _ffn(xt, norm_g[layer, 0], ffn_w_in[layer, 0], ffn_w_out[layer, 0])
        idx = layer // 2
        if layer % 2 == 0:
            xt = _conv(xt, norm_g[layer, 1], conv_w_pw1[idx], conv_b_pw1[idx], conv_w_dw[idx], conv_b_dw[idx],
                       conv_norm_g[idx], conv_w_pw2[idx], conv_b_pw2[idx], batch, seq)
        else:
            xt = _attention_mixer(xt, norm_g[layer, 1], attn_w_qkv[idx], attn_q_norm[idx], attn_k_norm[idx],
                                  attn_w_o[idx], batch, seq)
        xt = _ffn(xt, norm_g[layer, 2], ffn_w_in[layer, 1], ffn_w_out[layer, 1])
    return xt.reshape(batch, seq, D_MODEL)
```

```python
import functools

import jax
import jax.numpy as jnp
from jax import lax
from jax.experimental import pallas as pl
from jax.experimental.pallas import tpu as pltpu

D_MODEL = 1024
D_FF = 2816
CONV_WIDTH = 31
ATTN_GROUPS = ((128, 1), (512, 4), (2048, 16))
HEADS = 8
HEAD_DIM = 128
ATTN_BLOCK = 128
NORM_EPS = 1e-6
MASKED = -1e30
LANES = 128
CONV_HALO = 32

F32 = jnp.float32
BF16 = jnp.bfloat16

FFN_TOKENS = 1024
FFN_CHUNK = 256
CONV_TOKENS = 512
CONV_ROWS = 64
QKV_TOKENS = 1024
MERGE_TOKENS = 512
ATTN_TILING = ((4, 8, 1), (2, 8, 1), (1, 4, 4))
VMEM_LIMIT = 56 * 1024 * 1024


def _resident(shape):
    return pl.BlockSpec(shape, lambda *_: (0,) * len(shape), pipeline_mode=pl.Buffered(1))


def _params(*semantics):
    return pltpu.CompilerParams(dimension_semantics=semantics, vmem_limit_bytes=VMEM_LIMIT)


def _rmsnorm(x, g):
    return x * lax.rsqrt(jnp.mean(x * x, axis=-1, keepdims=True) + NORM_EPS) * g


def _sigmoid(x):
    return 1.0 / (1.0 + jnp.exp(-x))


def _swiglu_residual(x, g_ref, win_ref, wout_ref, h_ref):
    h_ref[...] = _rmsnorm(x, g_ref[...]).astype(BF16)
    acc = jnp.zeros(x.shape, F32)
    for c in range(D_FF // FFN_CHUNK):
        lo = c * FFN_CHUNK
        h = h_ref[...]
        gate = jnp.dot(h, win_ref[:, lo:lo + FFN_CHUNK], preferred_element_type=F32)
        up = jnp.dot(h, win_ref[:, D_FF + lo:D_FF + lo + FFN_CHUNK], preferred_element_type=F32)
        a = (gate * _sigmoid(gate) * up).astype(BF16)
        acc = acc + jnp.dot(a, wout_ref[lo:lo + FFN_CHUNK, :], preferred_element_type=F32)
    return x + 0.5 * acc


def _ffn_kernel(x_ref, g_ref, win_ref, wout_ref, o_ref, h_ref):
    o_ref[...] = _swiglu_residual(x_ref[...], g_ref, win_ref, wout_ref, h_ref)


def _ffn(x, g, w_in, w_out):
    tokens = x.shape[0]
    return pl.pallas_call(
        _ffn_kernel,
        grid=(tokens // FFN_TOKENS,),
        in_specs=[
            pl.BlockSpec((FFN_TOKENS, D_MODEL), lambda i: (i, 0)),
            _resident((1, D_MODEL)),
            _resident((D_MODEL, 2 * D_FF)),
            _resident((D_FF, D_MODEL)),
        ],
        out_specs=pl.BlockSpec((FFN_TOKENS, D_MODEL), lambda i: (i, 0)),
        out_shape=jax.ShapeDtypeStruct(x.shape, F32),
        scratch_shapes=[pltpu.VMEM((FFN_TOKENS, D_MODEL), BF16)],
        compiler_params=_params("parallel"),
        name="ffn",
    )(x, g.reshape(1, D_MODEL), w_in.astype(BF16), w_out.astype(BF16))


def _conv_kernel(x_ref, g_ref, w1_ref, b1_ref, wdw_ref, bdw_ref, gc_ref, w2_ref, b2_ref, o_ref,
                 u_ref, y_ref):
    ts = x_ref.shape[0]
    slabs = D_MODEL // LANES

    @pl.when(pl.program_id(1) == 0)
    def _():
        u_ref[:, 0:CONV_HALO, :] = jnp.zeros((slabs, CONV_HALO, LANES), F32)

    x = x_ref[...]
    h = _rmsnorm(x, g_ref[...]).astype(BF16)
    ag = jnp.dot(h, w1_ref[...], preferred_element_type=F32) + b1_ref[...]
    u = ag[:, :D_MODEL] * _sigmoid(ag[:, D_MODEL:])
    for cb in range(slabs):
        u_ref[cb, CONV_HALO:CONV_HALO + ts, :] = u[:, cb * LANES:(cb + 1) * LANES]

    first = CONV_HALO - (CONV_WIDTH - 1)
    for cb in range(slabs):
        cols = pl.ds(cb * LANES, LANES)
        w = wdw_ref[:, cols]
        b = bdw_ref[:, cols]
        for parity in range(2):
            for rc in range(ts // (2 * CONV_ROWS)):
                base = parity + 2 * rc * CONV_ROWS
                acc = jnp.broadcast_to(b, (CONV_ROWS, LANES))
                for k in range(CONV_WIDTH):
                    acc = acc + w[k:k + 1, :] * u_ref[cb, pl.ds(first + k + base, CONV_ROWS, stride=2), :]
                y_ref[cb, pl.ds(base, CONV_ROWS, stride=2), :] = acc

    for cb in range(slabs):
        u_ref[cb, 0:CONV_HALO, :] = u_ref[cb, ts:ts + CONV_HALO, :]

    y = _rmsnorm(jnp.concatenate([y_ref[cb] for cb in range(slabs)], axis=1), gc_ref[...])
    y = (y * _sigmoid(y)).astype(BF16)
    o_ref[...] = x + jnp.dot(y, w2_ref[...], preferred_element_type=F32) + b2_ref[...]


def _conv(x, g, w_pw1, b_pw1, w_dw, b_dw, g_conv, w_pw2, b_pw2, batch, seq):
    ts = CONV_TOKENS
    x3 = x.reshape(batch, seq, D_MODEL)
    tile = pl.BlockSpec((None, ts, D_MODEL), lambda b, s: (b, s, 0))
    out = pl.pallas_call(
        _conv_kernel,
        grid=(batch, seq // ts),
        in_specs=[
            tile,
            _resident((1, D_MODEL)),
            _resident((D_MODEL, 2 * D_MODEL)),
            _resident((1, 2 * D_MODEL)),
            _resident((CONV_WIDTH, D_MODEL)),
            _resident((1, D_MODEL)),
            _resident((1, D_MODEL)),
            _resident((D_MODEL, D_MODEL)),
            _resident((1, D_MODEL)),
        ],
        out_specs=tile,
        out_shape=jax.ShapeDtypeStruct(x3.shape, F32),
        scratch_shapes=[pltpu.VMEM((D_MODEL // LANES, CONV_HALO + ts, LANES), F32),
                        pltpu.VMEM((D_MODEL // LANES, ts, LANES), F32)],
        compiler_params=_params("parallel", "arbitrary"),
        name="conv",
    )(x3, g.reshape(1, D_MODEL), w_pw1.astype(BF16), b_pw1.reshape(1, -1), w_dw, b_dw.reshape(1, -1),
      g_conv.reshape(1, -1), w_pw2.astype(BF16), b_pw2.reshape(1, -1))
    return out.reshape(x.shape)


def _qkv_kernel(x_ref, g_ref, w_ref, gain_ref, o_ref, h_ref, *stage, dilation):
    tm = x_ref.shape[0]
    rows = tm // dilation
    h = _rmsnorm(x_ref[...], g_ref[...])
    if dilation == 1:
        h_ref[...] = h.astype(BF16)
    else:
        slab = stage[0]
        for cb in range(D_MODEL // LANES):
            slab[cb] = h[:, cb * LANES:(cb + 1) * LANES]
        for cb in range(D_MODEL // LANES):
            for r in range(dilation):
                h_ref[r * rows:(r + 1) * rows, cb * LANES:(cb + 1) * LANES] = (
                    slab[cb, pl.ds(r, rows, stride=dilation), :].astype(BF16))

    pair = 2 * HEAD_DIM
    for j in range(3):
        for c in range(D_MODEL // pair):
            lo = j * D_MODEL + c * pair
            acc = jnp.dot(h_ref[...], w_ref[:, lo:lo + pair], preferred_element_type=F32)
            for sub in range(2):
                head = 2 * c + sub
                blk = acc[:, sub * HEAD_DIM:(sub + 1) * HEAD_DIM]
                if j < 2:
                    blk = _rmsnorm(blk, gain_ref[j:j + 1, head * HEAD_DIM:(head + 1) * HEAD_DIM])
                val = blk.astype(BF16)
                for r in range(dilation):
                    o_ref[j, head, r] = val[r * rows:(r + 1) * rows]


def _qkv(x, g, w, gains, dilation, batch, seq):
    tm = QKV_TOKENS
    tiles = seq // tm
    rows = tm // dilation
    scratch = [pltpu.VMEM((tm, D_MODEL), BF16)]
    if dilation > 1:
        scratch.append(pltpu.VMEM((D_MODEL // LANES, tm, LANES), F32))
    return pl.pallas_call(
        functools.partial(_qkv_kernel, dilation=dilation),
        grid=(batch * tiles,),
        in_specs=[
            pl.BlockSpec((tm, D_MODEL), lambda i: (i, 0)),
            _resident((1, D_MODEL)),
            _resident((D_MODEL, 3 * D_MODEL)),
            _resident((2, D_MODEL)),
        ],
        out_specs=pl.BlockSpec((None, 3, HEADS, dilation, rows, HEAD_DIM),
                               lambda i: (i // tiles, 0, 0, 0, i % tiles, 0)),
        out_shape=jax.ShapeDtypeStruct((batch, 3, HEADS, dilation, seq // dilation, HEAD_DIM), BF16),
        scratch_shapes=scratch,
        compiler_params=_params("parallel"),
        name=f"qkv_d{dilation}",
    )(x, g.reshape(1, D_MODEL), w, gains)


def _attn_kernel(q_ref, kc_ref, vc_ref, kp_ref, vp_ref, o_ref, lse_ref, *, dilation, blocks, heads, unroll):
    qb = ATTN_BLOCK
    row = lax.broadcasted_iota(jnp.int32, (qb, 2 * qb), 0)
    col = lax.broadcasted_iota(jnp.int32, (qb, 2 * qb), 1)
    bias = jnp.where((col >= row) & (col <= row + qb), 0.0, MASKED).astype(F32)
    bias_first = jnp.where((pl.program_id(1) == 0) & (col < qb), MASKED, bias)
    lane = lax.broadcasted_iota(jnp.int32, (qb, LANES), 1)
    head0 = pl.program_id(2) * heads
    ones = jnp.ones((2 * qb, HEAD_DIM), BF16)
    nt = (((1,), (1,)), ((), ()))

    def stream(r):
        for b in range(blocks):
            cur = pl.ds(b * qb, qb)
            if dilation == 1:
                out_rows = cur
            else:
                out_rows = pl.ds(r + b * qb * dilation, qb, stride=dilation)
            lse_all = jnp.zeros((qb, LANES), F32)
            for h in range(heads):
                q = q_ref[h, r, cur, :]
                if b == 0:
                    k2 = jnp.concatenate([kp_ref[h, r], kc_ref[h, r, cur, :]], axis=0)
                    v2 = jnp.concatenate([vp_ref[h, r], vc_ref[h, r, cur, :]], axis=0)
                    bb = bias_first
                else:
                    both = pl.ds((b - 1) * qb, 2 * qb)
                    k2, v2, bb = kc_ref[h, r, both, :], vc_ref[h, r, both, :], bias
                s = lax.dot_general(q, k2, nt, preferred_element_type=F32) + bb
                m = jnp.max(s, axis=-1, keepdims=True)
                p = jnp.exp(s - m).astype(BF16)
                oa = jnp.dot(p, jnp.concatenate([v2, ones], axis=1), preferred_element_type=F32)
                den = oa[:, HEAD_DIM:]
                o_ref[h, out_rows, :] = oa[:, :HEAD_DIM] / den
                lse_all = jnp.where(lane == head0 + h, m + jnp.log(den), lse_all)
            lse_ref[out_rows, :] = lse_all

    def body(it, carry):
        for sub in range(unroll):
            stream(it * unroll + sub)
        return carry

    if dilation == unroll:
        body(0, 0)
    else:
        lax.fori_loop(0, dilation // unroll, body, 0)


def _attention(qkv, dilation, blocks, heads, unroll):
    batch, _, _, _, length, _ = qkv.shape
    rb = blocks * ATTN_BLOCK
    seq = length * dilation
    steps = length // rb

    def cur(which):
        return pl.BlockSpec((None, None, heads, dilation, rb, HEAD_DIM),
                            lambda b, n, hs: (b, which, hs, 0, n, 0))

    def prev(which):
        return pl.BlockSpec((None, None, heads, dilation, ATTN_BLOCK, HEAD_DIM),
                            lambda b, n, hs: (b, which, hs, 0, jnp.maximum(n * blocks - 1, 0), 0))

    return pl.pallas_call(
        functools.partial(_attn_kernel, dilation=dilation, blocks=blocks, heads=heads, unroll=unroll),
        grid=(batch, steps, HEADS // heads),
        in_specs=[cur(0), cur(1), cur(2), prev(1), prev(2)],
        out_specs=[
            pl.BlockSpec((None, heads, rb * dilation, HEAD_DIM), lambda b, n, hs: (b, hs, n, 0)),
            pl.BlockSpec((None, None, rb * dilation, LANES), lambda b, n, hs: (b, hs, n, 0)),
        ],
        out_shape=[
            jax.ShapeDtypeStruct((batch, HEADS, seq, HEAD_DIM), F32),
            jax.ShapeDtypeStruct((batch, HEADS // heads, seq, LANES), F32),
        ],
        compiler_params=_params("parallel", "parallel", "parallel"),
        name=f"attn_d{dilation}",
    )(qkv, qkv, qkv, qkv, qkv)


def _merge_ffn_kernel(x_ref, o0_ref, o1_ref, o2_ref, l0_ref, l1_ref, l2_ref, wo_ref, g_ref, win_ref, wout_ref,
                      out_ref, h_ref):
    def per_head(l_ref):
        l = l_ref[0]
        for t in range(1, l_ref.shape[0]):
            l = l + l_ref[t]
        return l

    lses = [per_head(l_ref) for l_ref in (l0_ref, l1_ref, l2_ref)]
    top = jnp.maximum(jnp.maximum(lses[0], lses[1]), lses[2])
    row = lax.broadcasted_iota(jnp.int32, (LANES, D_MODEL), 0)
    col = lax.broadcasted_iota(jnp.int32, (LANES, D_MODEL), 1)
    spread = jnp.where((col >= row * HEAD_DIM) & (col < (row + 1) * HEAD_DIM), 1.0, 0.0).astype(BF16)
    mix = None
    total = None
    for lse, o_ref in zip(lses, (o0_ref, o1_ref, o2_ref)):
        w = jnp.dot(jnp.exp(lse - top).astype(BF16), spread, preferred_element_type=F32)
        o = jnp.concatenate([o_ref[h] for h in range(HEADS)], axis=1)
        mix = w * o if mix is None else mix + w * o
        total = w if total is None else total + w
    merged = (mix / total).astype(BF16)
    x = x_ref[...] + jnp.dot(merged, wo_ref[...], preferred_element_type=F32)
    out_ref[...] = _swiglu_residual(x, g_ref, win_ref, wout_ref, h_ref)


def _merge_ffn(x, outs, lses, w_o, g, w_in, w_out, batch, seq):
    tm = MERGE_TOKENS
    tiles = seq // tm
    o_spec = pl.BlockSpec((None, HEADS, tm, HEAD_DIM), lambda i: (i // tiles, 0, i % tiles, 0))
    l_specs = [pl.BlockSpec((None, l.shape[1], tm, LANES), lambda i: (i // tiles, 0, i % tiles, 0)) for l in lses]
    tile = pl.BlockSpec((tm, D_MODEL), lambda i: (i, 0))
    return pl.pallas_call(
        _merge_ffn_kernel,
        grid=(batch * tiles,),
        in_specs=[tile, o_spec, o_spec, o_spec, *l_specs, _resident((D_MODEL, D_MODEL)),
                  _resident((1, D_MODEL)), _resident((D_MODEL, 2 * D_FF)), _resident((D_FF, D_MODEL))],
        out_specs=tile,
        out_shape=jax.ShapeDtypeStruct(x.shape, F32),
        scratch_shapes=[pltpu.VMEM((tm, D_MODEL), BF16)],
        compiler_params=_params("parallel"),
        name="attn_merge_ffn",
    )(x, *outs, *lses, w_o.astype(BF16), g.reshape(1, D_MODEL), w_in.astype(BF16), w_out.astype(BF16))


def _attention_groups(x, g, w_qkv, q_norm, k_norm, batch, seq):
    cols = HEADS * HEAD_DIM
    outs, lses = [], []
    for grp, ((_, dilation), (blocks, heads, unroll)) in enumerate(zip(ATTN_GROUPS, ATTN_TILING)):
        w = w_qkv[:, grp * 3 * cols:(grp + 1) * 3 * cols].astype(BF16)
        gains = jnp.stack([jnp.tile(q_norm[grp] * HEAD_DIM ** -0.5, HEADS), jnp.tile(k_norm[grp], HEADS)])
        qkv = _qkv(x, g, w, gains, dilation, batch, seq)
        o, lse = _attention(qkv, dilation, blocks, heads, unroll)
        outs.append(o)
        lses.append(lse)
    return outs, lses


def kernel(x, norm_g, ffn_w_in, ffn_w_out, conv_w_pw1, conv_b_pw1, conv_w_dw, conv_b_dw, conv_norm_g,
           conv_w_pw2, conv_b_pw2, attn_w_qkv, attn_q_norm, attn_k_norm, attn_w_o):
    batch, seq, _ = x.shape
    xt = x.reshape(batch * seq, D_MODEL)
    for layer in range(norm_g.shape[0]):
        xt = _ffn(xt, norm_g[layer, 0], ffn_w_in[layer, 0], ffn_w_out[layer, 0])
        idx = layer // 2
        if layer % 2 == 0:
            xt = _conv(xt, norm_g[layer, 1], conv_w_pw1[idx], conv_b_pw1[idx], conv_w_dw[idx], conv_b_dw[idx],
                       conv_norm_g[idx], conv_w_pw2[idx], conv_b_pw2[idx], batch, seq)
            xt = _ffn(xt, norm_g[layer, 2], ffn_w_in[layer, 1], ffn_w_out[layer, 1])
        else:
            outs, lses = _attention_groups(xt, norm_g[layer, 1], attn_w_qkv[idx], attn_q_norm[idx],
                                           attn_k_norm[idx], batch, seq)
            xt = _merge_ffn(xt, outs, lses, attn_w_o[idx], norm_g[layer, 2], ffn_w_in[layer, 1],
                            ffn_w_out[layer, 1], batch, seq)
    return xt.reshape(batch, seq, D_MODEL)
```

```python
import functools

import jax
import jax.numpy as jnp
from jax import lax
from jax.experimental import pallas as pl
from jax.experimental.pallas import tpu as pltpu

D_MODEL = 1024
D_FF = 2816
CONV_WIDTH = 31
ATTN_GROUPS = ((128, 1), (512, 4), (2048, 16))
HEADS = 8
HEAD_DIM = 128
ATTN_BLOCK = 128
NORM_EPS = 1e-6
MASKED = -1e30
LANES = 128
CONV_HALO = 32

F32 = jnp.float32
BF16 = jnp.bfloat16

FFN_TOKENS = 1024
FFN_CHUNK = 256
CONV_TOKENS = 512
CONV_ROWS = 64
QKV_TOKENS = 1024
QKV_STRIDE = 4
MERGE_TOKENS = 512
ATTN_TILING = ((4, 8, 1), (2, 8, 1), (1, 4, 4))
VMEM_LIMIT = 56 * 1024 * 1024


def _resident(shape):
    return pl.BlockSpec(shape, lambda *_: (0,) * len(shape), pipeline_mode=pl.Buffered(1))


def _resident_at(lead, shape):
    return pl.BlockSpec((None,) * len(lead) + shape, lambda *_: lead + (0,) * len(shape),
                        pipeline_mode=pl.Buffered(1))


def _params(*semantics):
    return pltpu.CompilerParams(dimension_semantics=semantics, vmem_limit_bytes=VMEM_LIMIT)


def _rmsnorm(x, g):
    return x * lax.rsqrt(jnp.mean(x * x, axis=-1, keepdims=True) + NORM_EPS) * g


def _sigmoid(x):
    return 1.0 / (1.0 + jnp.exp(-x))


def _swiglu_residual(x, g_ref, win_ref, wout_ref, h_ref):
    h_ref[...] = _rmsnorm(x, g_ref[...]).astype(BF16)
    acc = jnp.zeros(x.shape, F32)
    for c in range(D_FF // FFN_CHUNK):
        lo = c * FFN_CHUNK
        h = h_ref[...]
        gate = jnp.dot(h, win_ref[:, lo:lo + FFN_CHUNK], preferred_element_type=F32)
        up = jnp.dot(h, win_ref[:, D_FF + lo:D_FF + lo + FFN_CHUNK], preferred_element_type=F32)
        a = (gate * _sigmoid(gate) * up).astype(BF16)
        acc = acc + jnp.dot(a, wout_ref[lo:lo + FFN_CHUNK, :], preferred_element_type=F32)
    return x + 0.5 * acc


def _ffn_kernel(x_ref, g_ref, win_ref, wout_ref, o_ref, h_ref):
    o_ref[...] = _swiglu_residual(x_ref[...], g_ref, win_ref, wout_ref, h_ref)


def _ffn(x, g, w_in, w_out, which):
    tokens = x.shape[0]
    return pl.pallas_call(
        _ffn_kernel,
        grid=(tokens // FFN_TOKENS,),
        in_specs=[
            pl.BlockSpec((FFN_TOKENS, D_MODEL), lambda i: (i, 0)),
            _resident((1, D_MODEL)),
            _resident_at(which, (D_MODEL, 2 * D_FF)),
            _resident_at(which, (D_FF, D_MODEL)),
        ],
        out_specs=pl.BlockSpec((FFN_TOKENS, D_MODEL), lambda i: (i, 0)),
        out_shape=jax.ShapeDtypeStruct(x.shape, F32),
        scratch_shapes=[pltpu.VMEM((FFN_TOKENS, D_MODEL), BF16)],
        compiler_params=_params("parallel"),
        name="ffn",
    )(x, g.reshape(1, D_MODEL), w_in, w_out)


def _conv_kernel(x_ref, g_ref, w1_ref, b1_ref, wdw_ref, bdw_ref, gc_ref, w2_ref, b2_ref, o_ref,
                 u_ref, y_ref):
    ts = x_ref.shape[0]
    slabs = D_MODEL // LANES

    @pl.when(pl.program_id(1) == 0)
    def _():
        u_ref[:, 0:CONV_HALO, :] = jnp.zeros((slabs, CONV_HALO, LANES), F32)

    x = x_ref[...]
    h = _rmsnorm(x, g_ref[...]).astype(BF16)
    ag = jnp.dot(h, w1_ref[...], preferred_element_type=F32) + b1_ref[...]
    u = ag[:, :D_MODEL] * _sigmoid(ag[:, D_MODEL:])
    for cb in range(slabs):
        u_ref[cb, CONV_HALO:CONV_HALO + ts, :] = u[:, cb * LANES:(cb + 1) * LANES]

    first = CONV_HALO - (CONV_WIDTH - 1)
    for cb in range(slabs):
        cols = pl.ds(cb * LANES, LANES)
        w = wdw_ref[:, cols]
        b = bdw_ref[:, cols]
        for parity in range(2):
            for rc in range(ts // (2 * CONV_ROWS)):
                base = parity + 2 * rc * CONV_ROWS
                acc = jnp.broadcast_to(b, (CONV_ROWS, LANES))
                for k in range(CONV_WIDTH):
                    acc = acc + w[k:k + 1, :] * u_ref[cb, pl.ds(first + k + base, CONV_ROWS, stride=2), :]
                y_ref[cb, pl.ds(base, CONV_ROWS, stride=2), :] = acc

    for cb in range(slabs):
        u_ref[cb, 0:CONV_HALO, :] = u_ref[cb, ts:ts + CONV_HALO, :]

    y = _rmsnorm(jnp.concatenate([y_ref[cb] for cb in range(slabs)], axis=1), gc_ref[...])
    y = (y * _sigmoid(y)).astype(BF16)
    o_ref[...] = x + jnp.dot(y, w2_ref[...], preferred_element_type=F32) + b2_ref[...]


def _conv(x, g, w_pw1, b_pw1, w_dw, b_dw, g_conv, w_pw2, b_pw2, batch, seq):
    ts = CONV_TOKENS
    x3 = x.reshape(batch, seq, D_MODEL)
    tile = pl.BlockSpec((None, ts, D_MODEL), lambda b, s: (b, s, 0))
    out = pl.pallas_call(
        _conv_kernel,
        grid=(batch, seq // ts),
        in_specs=[
            tile,
            _resident((1, D_MODEL)),
            _resident((D_MODEL, 2 * D_MODEL)),
            _resident((1, 2 * D_MODEL)),
            _resident((CONV_WIDTH, D_MODEL)),
            _resident((1, D_MODEL)),
            _resident((1, D_MODEL)),
            _resident((D_MODEL, D_MODEL)),
            _resident((1, D_MODEL)),
        ],
        out_specs=tile,
        out_shape=jax.ShapeDtypeStruct(x3.shape, F32),
        scratch_shapes=[pltpu.VMEM((D_MODEL // LANES, CONV_HALO + ts, LANES), F32),
                        pltpu.VMEM((D_MODEL // LANES, ts, LANES), F32)],
        compiler_params=_params("parallel", "arbitrary"),
        name="conv",
    )(x3, g.reshape(1, D_MODEL), w_pw1.astype(BF16), b_pw1.reshape(1, -1), w_dw, b_dw.reshape(1, -1),
      g_conv.reshape(1, -1), w_pw2.astype(BF16), b_pw2.reshape(1, -1))
    return out.reshape(x.shape)


def _qkv_kernel(x_ref, g_ref, w_ref, gain_ref, o_ref, h_ref, *stage, dilation):
    tm = x_ref.shape[0]
    rows = tm // dilation
    h = _rmsnorm(x_ref[...], g_ref[...])
    if dilation == 1:
        h_ref[...] = h.astype(BF16)
    else:
        slab = stage[0]
        for cb in range(D_MODEL // LANES):
            slab[cb] = h[:, cb * LANES:(cb + 1) * LANES]
        if dilation > QKV_STRIDE:
            rest = dilation // QKV_STRIDE
            coarse = tm // QKV_STRIDE
            for cb in range(D_MODEL // LANES):
                for r0 in range(QKV_STRIDE):
                    stage[1][cb, r0 * coarse:(r0 + 1) * coarse, :] = slab[cb, pl.ds(r0, coarse, stride=QKV_STRIDE), :]
            for cb in range(D_MODEL // LANES):
                for r0 in range(QKV_STRIDE):
                    for r1 in range(rest):
                        r = r0 + QKV_STRIDE * r1
                        h_ref[r * rows:(r + 1) * rows, cb * LANES:(cb + 1) * LANES] = (
                            stage[1][cb, pl.ds(r0 * coarse + r1, rows, stride=rest), :].astype(BF16))
        else:
            for cb in range(D_MODEL // LANES):
                for r in range(dilation):
                    h_ref[r * rows:(r + 1) * rows, cb * LANES:(cb + 1) * LANES] = (
                        slab[cb, pl.ds(r, rows, stride=dilation), :].astype(BF16))

    pair = 2 * HEAD_DIM
    for j in range(3):
        for c in range(D_MODEL // pair):
            lo = j * D_MODEL + c * pair
            acc = jnp.dot(h_ref[...], w_ref[:, lo:lo + pair], preferred_element_type=F32)
            for sub in range(2):
                head = 2 * c + sub
                blk = acc[:, sub * HEAD_DIM:(sub + 1) * HEAD_DIM]
                if j < 2:
                    blk = _rmsnorm(blk, gain_ref[j:j + 1, head * HEAD_DIM:(head + 1) * HEAD_DIM])
                val = blk.astype(BF16)
                for r in range(dilation):
                    o_ref[j, head, r] = val[r * rows:(r + 1) * rows]


def _qkv(x, g, w, grp, gains, dilation, batch, seq):
    tm = QKV_TOKENS
    tiles = seq // tm
    rows = tm // dilation
    scratch = [pltpu.VMEM((tm, D_MODEL), BF16)]
    if dilation > 1:
        scratch.append(pltpu.VMEM((D_MODEL // LANES, tm, LANES), F32))
    if dilation > QKV_STRIDE:
        scratch.append(pltpu.VMEM((D_MODEL // LANES, tm, LANES), F32))
    return pl.pallas_call(
        functools.partial(_qkv_kernel, dilation=dilation),
        grid=(batch * tiles,),
        in_specs=[
            pl.BlockSpec((tm, D_MODEL), lambda i: (i, 0)),
            _resident((1, D_MODEL)),
            pl.BlockSpec((D_MODEL, 3 * D_MODEL), lambda i: (0, grp), pipeline_mode=pl.Buffered(1)),
            _resident((2, D_MODEL)),
        ],
        out_specs=pl.BlockSpec((None, 3, HEADS, dilation, rows, HEAD_DIM),
                               lambda i: (i // tiles, 0, 0, 0, i % tiles, 0)),
        out_shape=jax.ShapeDtypeStruct((batch, 3, HEADS, dilation, seq // dilation, HEAD_DIM), BF16),
        scratch_shapes=scratch,
        compiler_params=_params("parallel"),
        name=f"qkv_d{dilation}",
    )(x, g.reshape(1, D_MODEL), w, gains)


def _attn_kernel(q_ref, kc_ref, vc_ref, kp_ref, vp_ref, o_ref, stat_ref, *, dilation, blocks, heads, unroll):
    qb = ATTN_BLOCK
    row = lax.broadcasted_iota(jnp.int32, (qb, 2 * qb), 0)
    col = lax.broadcasted_iota(jnp.int32, (qb, 2 * qb), 1)
    bias = jnp.where((col >= row) & (col <= row + qb), 0.0, MASKED).astype(F32)
    bias_first = jnp.where((pl.program_id(1) == 0) & (col < qb), MASKED, bias)
    lane = lax.broadcasted_iota(jnp.int32, (qb, LANES), 1)
    head0 = pl.program_id(2) * heads
    ones = jnp.ones((2 * qb, HEAD_DIM), BF16)
    nt = (((1,), (1,)), ((), ()))

    def stream(r):
        for b in range(blocks):
            cur = pl.ds(b * qb, qb)
            if dilation == 1:
                out_rows = cur
            else:
                out_rows = pl.ds(r + b * qb * dilation, qb, stride=dilation)
            stats = jnp.zeros((qb, LANES), F32)
            for h in range(heads):
                q = q_ref[h, r, cur, :]
                if b == 0:
                    k2 = jnp.concatenate([kp_ref[h, r], kc_ref[h, r, cur, :]], axis=0)
                    v2 = jnp.concatenate([vp_ref[h, r], vc_ref[h, r, cur, :]], axis=0)
                    bb = bias_first
                else:
                    both = pl.ds((b - 1) * qb, 2 * qb)
                    k2, v2, bb = kc_ref[h, r, both, :], vc_ref[h, r, both, :], bias
                s = lax.dot_general(q, k2, nt, preferred_element_type=F32) + bb
                m = jnp.max(s, axis=-1, keepdims=True)
                p = jnp.exp(s - m).astype(BF16)
                oa = jnp.dot(p, jnp.concatenate([v2, ones], axis=1), preferred_element_type=F32)
                o_ref[h, out_rows, :] = oa[:, :HEAD_DIM].astype(o_ref.dtype)
                stats = jnp.where(lane == head0 + h, m, stats)
                stats = jnp.where(lane == HEADS + head0 + h, oa[:, HEAD_DIM:], stats)
            stat_ref[out_rows, :] = stats

    def body(it, carry):
        for sub in range(unroll):
            stream(it * unroll + sub)
        return carry

    if dilation == unroll:
        body(0, 0)
    else:
        lax.fori_loop(0, dilation // unroll, body, 0)


def _attention(qkv, dilation, blocks, heads, unroll):
    batch, _, _, _, length, _ = qkv.shape
    rb = blocks * ATTN_BLOCK
    seq = length * dilation
    steps = length // rb

    def cur(which):
        return pl.BlockSpec((None, None, heads, dilation, rb, HEAD_DIM),
                            lambda b, n, hs: (b, which, hs, 0, n, 0))

    def prev(which):
        return pl.BlockSpec((None, None, heads, dilation, ATTN_BLOCK, HEAD_DIM),
                            lambda b, n, hs: (b, which, hs, 0, jnp.maximum(n * blocks - 1, 0), 0))

    return pl.pallas_call(
        functools.partial(_attn_kernel, dilation=dilation, blocks=blocks, heads=heads, unroll=unroll),
        grid=(batch, steps, HEADS // heads),
        in_specs=[cur(0), cur(1), cur(2), prev(1), prev(2)],
        out_specs=[
            pl.BlockSpec((None, heads, rb * dilation, HEAD_DIM), lambda b, n, hs: (b, hs, n, 0)),
            pl.BlockSpec((None, None, rb * dilation, LANES), lambda b, n, hs: (b, hs, n, 0)),
        ],
        out_shape=[
            jax.ShapeDtypeStruct((batch, HEADS, seq, HEAD_DIM), BF16 if dilation == 1 else F32),
            jax.ShapeDtypeStruct((batch, HEADS // heads, seq, LANES), F32),
        ],
        compiler_params=_params("parallel", "parallel", "parallel"),
        name=f"attn_d{dilation}",
    )(qkv, qkv, qkv, qkv, qkv)


def _merge_ffn_kernel(x_ref, o0_ref, o1_ref, o2_ref, s0_ref, s1_ref, s2_ref, wo_ref, g_ref, win_ref, wout_ref,
                      out_ref, h_ref):
    def per_head(s_ref):
        s = s_ref[0]
        for t in range(1, s_ref.shape[0]):
            s = s + s_ref[t]
        return s

    stats = [per_head(s_ref) for s_ref in (s0_ref, s1_ref, s2_ref)]
    top = jnp.maximum(jnp.maximum(stats[0], stats[1]), stats[2])
    e = [jnp.exp(s - top) for s in stats]
    den = [pltpu.roll(s, LANES - HEADS, axis=1) for s in stats]
    is_head = lax.broadcasted_iota(jnp.int32, top.shape, 1) < HEADS
    total = jnp.where(is_head, e[0] * den[0] + e[1] * den[1] + e[2] * den[2], 1.0)
    row = lax.broadcasted_iota(jnp.int32, (LANES, D_MODEL), 0)
    col = lax.broadcasted_iota(jnp.int32, (LANES, D_MODEL), 1)
    spread = jnp.where((col >= row * HEAD_DIM) & (col < (row + 1) * HEAD_DIM), 1.0, 0.0).astype(BF16)
    mix = None
    for e_g, o_ref in zip(e, (o0_ref, o1_ref, o2_ref)):
        w = jnp.dot(jnp.where(is_head, e_g / total, 0.0).astype(BF16), spread, preferred_element_type=F32)
        o = jnp.concatenate([o_ref[h].astype(F32) for h in range(HEADS)], axis=1)
        mix = w * o if mix is None else mix + w * o
    merged = mix.astype(BF16)
    x = x_ref[...] + jnp.dot(merged, wo_ref[...], preferred_element_type=F32)
    out_ref[...] = _swiglu_residual(x, g_ref, win_ref, wout_ref, h_ref)


def _merge_ffn(x, outs, stats, w_o, g, w_in, w_out, which, batch, seq):
    tm = MERGE_TOKENS
    tiles = seq // tm
    o_spec = pl.BlockSpec((None, HEADS, tm, HEAD_DIM), lambda i: (i // tiles, 0, i % tiles, 0))
    s_specs = [pl.BlockSpec((None, s.shape[1], tm, LANES), lambda i: (i // tiles, 0, i % tiles, 0)) for s in stats]
    tile = pl.BlockSpec((tm, D_MODEL), lambda i: (i, 0))
    return pl.pallas_call(
        _merge_ffn_kernel,
        grid=(batch * tiles,),
        in_specs=[tile, o_spec, o_spec, o_spec, *s_specs, _resident((D_MODEL, D_MODEL)),
                  _resident((1, D_MODEL)), _resident_at(which, (D_MODEL, 2 * D_FF)),
                  _resident_at(which, (D_FF, D_MODEL))],
        out_specs=tile,
        out_shape=jax.ShapeDtypeStruct(x.shape, F32),
        scratch_shapes=[pltpu.VMEM((tm, D_MODEL), BF16)],
        compiler_params=_params("parallel"),
        name="attn_merge_ffn",
    )(x, *outs, *stats, w_o.astype(BF16), g.reshape(1, D_MODEL), w_in, w_out)


def _attention_groups(x, g, w_qkv, q_norm, k_norm, batch, seq):
    w = w_qkv.astype(BF16)
    outs, stats = [], []
    for grp, ((_, dilation), (blocks, heads, unroll)) in enumerate(zip(ATTN_GROUPS, ATTN_TILING)):
        gains = jnp.stack([jnp.tile(q_norm[grp] * HEAD_DIM ** -0.5, HEADS), jnp.tile(k_norm[grp], HEADS)])
        qkv = _qkv(x, g, w, grp, gains, dilation, batch, seq)
        o, s = _attention(qkv, dilation, blocks, heads, unroll)
        outs.append(o)
        stats.append(s)
    return outs, stats


def kernel(x, norm_g, ffn_w_in, ffn_w_out, conv_w_pw1, conv_b_pw1, conv_w_dw, conv_b_dw, conv_norm_g,
           conv_w_pw2, conv_b_pw2, attn_w_qkv, attn_q_norm, attn_k_norm, attn_w_o):
    batch, seq, _ = x.shape
    xt = x.reshape(batch * seq, D_MODEL)
    w_in = ffn_w_in.astype(BF16)
    w_out = ffn_w_out.astype(BF16)
    for layer in range(norm_g.shape[0]):
        xt = _ffn(xt, norm_g[layer, 0], w_in, w_out, (layer, 0))
        idx = layer // 2
        if layer % 2 == 0:
            xt = _conv(xt, norm_g[layer, 1], conv_w_pw1[idx], conv_b_pw1[idx], conv_w_dw[idx], conv_b_dw[idx],
                       conv_norm_g[idx], conv_w_pw2[idx], conv_b_pw2[idx], batch, seq)
            xt = _ffn(xt, norm_g[layer, 2], w_in, w_out, (layer, 1))
        else:
            outs, stats = _attention_groups(xt, norm_g[layer, 1], attn_w_qkv[idx], attn_q_norm[idx],
                                            attn_k_norm[idx], batch, seq)
            xt = _merge_ffn(xt, outs, stats, attn_w_o[idx], norm_g[layer, 2], w_in, w_out, (layer, 1),
                            batch, seq)
    return xt.reshape(batch, seq, D_MODEL)
```

```python
import functools

import jax
import jax.numpy as jnp
from jax import lax
from jax.experimental import pallas as pl
from jax.experimental.pallas import tpu as pltpu

D_MODEL = 1024
D_FF = 2816
CONV_WIDTH = 31
ATTN_GROUPS = ((128, 1), (512, 4), (2048, 16))
HEADS = 8
HEAD_DIM = 128
ATTN_BLOCK = 128
NORM_EPS = 1e-6
MASKED = -1e30
LANES = 128
CONV_HALO = 32

F32 = jnp.float32
BF16 = jnp.bfloat16

FFN_TOKENS = 1024
FFN_CHUNK = 256
CONV_TOKENS = 512
CONV_ROWS = 64
QKV_TOKENS = 1024
QKV_STRIDE = 4
MERGE_TOKENS = 512
ATTN_TILING = ((8, 8, 1), (2, 8, 1), (1, 4, 4))
VMEM_LIMIT = 56 * 1024 * 1024


def _resident(shape):
    return pl.BlockSpec(shape, lambda *_: (0,) * len(shape), pipeline_mode=pl.Buffered(1))


def _resident_at(lead, shape):
    return pl.BlockSpec((None,) * len(lead) + shape, lambda *_: lead + (0,) * len(shape),
                        pipeline_mode=pl.Buffered(1))


def _params(*semantics):
    return pltpu.CompilerParams(dimension_semantics=semantics, vmem_limit_bytes=VMEM_LIMIT)


def _rmsnorm(x, g):
    return x * lax.rsqrt(jnp.mean(x * x, axis=-1, keepdims=True) + NORM_EPS) * g


def _sigmoid(x):
    return 1.0 / (1.0 + jnp.exp(-x))


def _swiglu_residual(x, g_ref, win_ref, wout_ref, h_ref):
    h_ref[...] = _rmsnorm(x, g_ref[...]).astype(BF16)
    acc = jnp.zeros(x.shape, F32)
    for c in range(D_FF // FFN_CHUNK):
        lo = c * FFN_CHUNK
        h = h_ref[...]
        gate = jnp.dot(h, win_ref[:, lo:lo + FFN_CHUNK], preferred_element_type=F32)
        up = jnp.dot(h, win_ref[:, D_FF + lo:D_FF + lo + FFN_CHUNK], preferred_element_type=F32)
        a = (gate * _sigmoid(gate) * up).astype(BF16)
        acc = acc + jnp.dot(a, wout_ref[lo:lo + FFN_CHUNK, :], preferred_element_type=F32)
    return x + 0.5 * acc


def _ffn_kernel(x_ref, g_ref, win_ref, wout_ref, o_ref, h_ref):
    o_ref[...] = _swiglu_residual(x_ref[...], g_ref, win_ref, wout_ref, h_ref)


def _ffn(x, g, w_in, w_out, which):
    tokens = x.shape[0]
    return pl.pallas_call(
        _ffn_kernel,
        grid=(tokens // FFN_TOKENS,),
        in_specs=[
            pl.BlockSpec((FFN_TOKENS, D_MODEL), lambda i: (i, 0)),
            _resident((1, D_MODEL)),
            _resident_at(which, (D_MODEL, 2 * D_FF)),
            _resident_at(which, (D_FF, D_MODEL)),
        ],
        out_specs=pl.BlockSpec((FFN_TOKENS, D_MODEL), lambda i: (i, 0)),
        out_shape=jax.ShapeDtypeStruct(x.shape, F32),
        scratch_shapes=[pltpu.VMEM((FFN_TOKENS, D_MODEL), BF16)],
        compiler_params=_params("parallel"),
        name="ffn",
    )(x, g.reshape(1, D_MODEL), w_in, w_out)


def _conv_kernel(x_ref, g_ref, w1_ref, b1_ref, wdw_ref, bdw_ref, gc_ref, w2_ref, b2_ref, o_ref,
                 u_ref, y_ref):
    ts = x_ref.shape[0]
    slabs = D_MODEL // LANES

    @pl.when(pl.program_id(1) == 0)
    def _():
        u_ref[:, 0:CONV_HALO, :] = jnp.zeros((slabs, CONV_HALO, LANES), F32)

    x = x_ref[...]
    h = _rmsnorm(x, g_ref[...]).astype(BF16)
    ag = jnp.dot(h, w1_ref[...], preferred_element_type=F32) + b1_ref[...]
    u = ag[:, :D_MODEL] * _sigmoid(ag[:, D_MODEL:])
    for cb in range(slabs):
        u_ref[cb, CONV_HALO:CONV_HALO + ts, :] = u[:, cb * LANES:(cb + 1) * LANES]

    first = CONV_HALO - (CONV_WIDTH - 1)
    for cb in range(slabs):
        cols = pl.ds(cb * LANES, LANES)
        w = wdw_ref[:, cols]
        b = bdw_ref[:, cols]
        for parity in range(2):
            for rc in range(ts // (2 * CONV_ROWS)):
                base = parity + 2 * rc * CONV_ROWS
                acc = jnp.broadcast_to(b, (CONV_ROWS, LANES))
                for k in range(CONV_WIDTH):
                    acc = acc + w[k:k + 1, :] * u_ref[cb, pl.ds(first + k + base, CONV_ROWS, stride=2), :]
                y_ref[cb, pl.ds(base, CONV_ROWS, stride=2), :] = acc

    for cb in range(slabs):
        u_ref[cb, 0:CONV_HALO, :] = u_ref[cb, ts:ts + CONV_HALO, :]

    y = _rmsnorm(jnp.concatenate([y_ref[cb] for cb in range(slabs)], axis=1), gc_ref[...])
    y = (y * _sigmoid(y)).astype(BF16)
    o_ref[...] = x + jnp.dot(y, w2_ref[...], preferred_element_type=F32) + b2_ref[...]


def _conv(x, g, w_pw1, b_pw1, w_dw, b_dw, g_conv, w_pw2, b_pw2, batch, seq):
    ts = CONV_TOKENS
    x3 = x.reshape(batch, seq, D_MODEL)
    tile = pl.BlockSpec((None, ts, D_MODEL), lambda b, s: (b, s, 0))
    out = pl.pallas_call(
        _conv_kernel,
        grid=(batch, seq // ts),
        in_specs=[
            tile,
            _resident((1, D_MODEL)),
            _resident((D_MODEL, 2 * D_MODEL)),
            _resident((1, 2 * D_MODEL)),
            _resident((CONV_WIDTH, D_MODEL)),
            _resident((1, D_MODEL)),
            _resident((1, D_MODEL)),
            _resident((D_MODEL, D_MODEL)),
            _resident((1, D_MODEL)),
        ],
        out_specs=tile,
        out_shape=jax.ShapeDtypeStruct(x3.shape, F32),
        scratch_shapes=[pltpu.VMEM((D_MODEL // LANES, CONV_HALO + ts, LANES), F32),
                        pltpu.VMEM((D_MODEL // LANES, ts, LANES), F32)],
        compiler_params=_params("parallel", "arbitrary"),
        name="conv",
    )(x3, g.reshape(1, D_MODEL), w_pw1.astype(BF16), b_pw1.reshape(1, -1), w_dw, b_dw.reshape(1, -1),
      g_conv.reshape(1, -1), w_pw2.astype(BF16), b_pw2.reshape(1, -1))
    return out.reshape(x.shape)


def _qkv_kernel(x_ref, g_ref, w_ref, gain_ref, o_ref, h_ref, *stage, dilation):
    tm = x_ref.shape[0]
    rows = tm // dilation
    h = _rmsnorm(x_ref[...], g_ref[...])
    if dilation == 1:
        h_ref[...] = h.astype(BF16)
    else:
        slab = stage[0]
        for cb in range(D_MODEL // LANES):
            slab[cb] = h[:, cb * LANES:(cb + 1) * LANES]
        if dilation > QKV_STRIDE:
            rest = dilation // QKV_STRIDE
            coarse = tm // QKV_STRIDE
            for cb in range(D_MODEL // LANES):
                for r0 in range(QKV_STRIDE):
                    stage[1][cb, r0 * coarse:(r0 + 1) * coarse, :] = slab[cb, pl.ds(r0, coarse, stride=QKV_STRIDE), :]
            for cb in range(D_MODEL // LANES):
                for r0 in range(QKV_STRIDE):
                    for r1 in range(rest):
                        r = r0 + QKV_STRIDE * r1
                        h_ref[r * rows:(r + 1) * rows, cb * LANES:(cb + 1) * LANES] = (
                            stage[1][cb, pl.ds(r0 * coarse + r1, rows, stride=rest), :].astype(BF16))
        else:
            for cb in range(D_MODEL // LANES):
                for r in range(dilation):
                    h_ref[r * rows:(r + 1) * rows, cb * LANES:(cb + 1) * LANES] = (
                        slab[cb, pl.ds(r, rows, stride=dilation), :].astype(BF16))

    pair = 2 * HEAD_DIM
    for j in range(3):
        for c in range(D_MODEL // pair):
            lo = j * D_MODEL + c * pair
            acc = jnp.dot(h_ref[...], w_ref[:, lo:lo + pair], preferred_element_type=F32)
            for sub in range(2):
                head = 2 * c + sub
                blk = acc[:, sub * HEAD_DIM:(sub + 1) * HEAD_DIM]
                if j < 2:
                    blk = _rmsnorm(blk, gain_ref[j:j + 1, head * HEAD_DIM:(head + 1) * HEAD_DIM])
                val = blk.astype(BF16)
                for r in range(dilation):
                    o_ref[j, head, r] = val[r * rows:(r + 1) * rows]


def _qkv(x, g, w, grp, gains, dilation, batch, seq):
    tm = QKV_TOKENS
    tiles = seq // tm
    rows = tm // dilation
    scratch = [pltpu.VMEM((tm, D_MODEL), BF16)]
    if dilation > 1:
        scratch.append(pltpu.VMEM((D_MODEL // LANES, tm, LANES), F32))
    if dilation > QKV_STRIDE:
        scratch.append(pltpu.VMEM((D_MODEL // LANES, tm, LANES), F32))
    return pl.pallas_call(
        functools.partial(_qkv_kernel, dilation=dilation),
        grid=(batch * tiles,),
        in_specs=[
            pl.BlockSpec((tm, D_MODEL), lambda i: (i, 0)),
            _resident((1, D_MODEL)),
            pl.BlockSpec((D_MODEL, 3 * D_MODEL), lambda i: (0, grp), pipeline_mode=pl.Buffered(1)),
            _resident((2, D_MODEL)),
        ],
        out_specs=pl.BlockSpec((None, 3, HEADS, dilation, rows, HEAD_DIM),
                               lambda i: (i // tiles, 0, 0, 0, i % tiles, 0)),
        out_shape=jax.ShapeDtypeStruct((batch, 3, HEADS, dilation, seq // dilation, HEAD_DIM), BF16),
        scratch_shapes=scratch,
        compiler_params=_params("parallel"),
        name=f"qkv_d{dilation}",
    )(x, g.reshape(1, D_MODEL), w, gains)


def _attn_kernel(q_ref, kc_ref, vc_ref, kp_ref, vp_ref, o_ref, stat_ref, *stage, dilation, blocks, heads, unroll):
    qb = ATTN_BLOCK
    o_f32 = stage[0] if dilation > 1 else None
    row = lax.broadcasted_iota(jnp.int32, (qb, 2 * qb), 0)
    col = lax.broadcasted_iota(jnp.int32, (qb, 2 * qb), 1)
    bias = jnp.where((col >= row) & (col <= row + qb), 0.0, MASKED).astype(F32)
    bias_first = jnp.where((pl.program_id(1) == 0) & (col < qb), MASKED, bias)
    lane = lax.broadcasted_iota(jnp.int32, (qb, LANES), 1)
    head0 = pl.program_id(2) * heads
    ones = jnp.ones((2 * qb, HEAD_DIM), BF16)
    nt = (((1,), (1,)), ((), ()))

    def stream(r):
        for b in range(blocks):
            cur = pl.ds(b * qb, qb)
            if dilation == 1:
                out_rows = cur
            else:
                out_rows = pl.ds(r + b * qb * dilation, qb, stride=dilation)
            stats = jnp.zeros((qb, LANES), F32)
            for h in range(heads):
                q = q_ref[h, r, cur, :]
                if b == 0:
                    k2 = jnp.concatenate([kp_ref[h, r], kc_ref[h, r, cur, :]], axis=0)
                    v2 = jnp.concatenate([vp_ref[h, r], vc_ref[h, r, cur, :]], axis=0)
                    bb = bias_first
                else:
                    both = pl.ds((b - 1) * qb, 2 * qb)
                    k2, v2, bb = kc_ref[h, r, both, :], vc_ref[h, r, both, :], bias
                s = lax.dot_general(q, k2, nt, preferred_element_type=F32) + bb
                m = jnp.max(s, axis=-1, keepdims=True)
                p = jnp.exp(s - m).astype(BF16)
                oa = jnp.dot(p, jnp.concatenate([v2, ones], axis=1), preferred_element_type=F32)
                if dilation == 1:
                    o_ref[h, out_rows, :] = oa[:, :HEAD_DIM].astype(BF16)
                else:
                    o_f32[h, out_rows, :] = oa[:, :HEAD_DIM]
                stats = jnp.where(lane == head0 + h, m, stats)
                stats = jnp.where(lane == HEADS + head0 + h, oa[:, HEAD_DIM:], stats)
            stat_ref[out_rows, :] = stats

    def body(it, carry):
        for sub in range(unroll):
            stream(it * unroll + sub)
        return carry

    if dilation == unroll:
        body(0, 0)
    else:
        lax.fori_loop(0, dilation // unroll, body, 0)
    if dilation > 1:
        for h in range(heads):
            o_ref[h] = o_f32[h].astype(BF16)


def _attention(qkv, dilation, blocks, heads, unroll):
    batch, _, _, _, length, _ = qkv.shape
    rb = blocks * ATTN_BLOCK
    seq = length * dilation
    steps = length // rb

    def cur(which):
        return pl.BlockSpec((None, None, heads, dilation, rb, HEAD_DIM),
                            lambda b, n, hs: (b, which, hs, 0, n, 0))

    def prev(which):
        return pl.BlockSpec((None, None, heads, dilation, ATTN_BLOCK, HEAD_DIM),
                            lambda b, n, hs: (b, which, hs, 0, jnp.maximum(n * blocks - 1, 0), 0))

    return pl.pallas_call(
        functools.partial(_attn_kernel, dilation=dilation, blocks=blocks, heads=heads, unroll=unroll),
        grid=(batch, steps, HEADS // heads),
        in_specs=[cur(0), cur(1), cur(2), prev(1), prev(2)],
        out_specs=[
            pl.BlockSpec((None, heads, rb * dilation, HEAD_DIM), lambda b, n, hs: (b, hs, n, 0)),
            pl.BlockSpec((None, None, rb * dilation, LANES), lambda b, n, hs: (b, hs, n, 0)),
        ],
        out_shape=[
            jax.ShapeDtypeStruct((batch, HEADS, seq, HEAD_DIM), BF16),
            jax.ShapeDtypeStruct((batch, HEADS // heads, seq, LANES), F32),
        ],
        scratch_shapes=[pltpu.VMEM((heads, rb * dilation, HEAD_DIM), F32)] if dilation > 1 else [],
        compiler_params=_params("parallel", "parallel", "parallel"),
        name=f"attn_d{dilation}",
    )(qkv, qkv, qkv, qkv, qkv)


def _merge_ffn_kernel(x_ref, o0_ref, o1_ref, o2_ref, s0_ref, s1_ref, s2_ref, wo_ref, g_ref, win_ref, wout_ref,
                      out_ref, h_ref):
    def per_head(s_ref):
        s = s_ref[0]
        for t in range(1, s_ref.shape[0]):
            s = s + s_ref[t]
        return s

    stats = [per_head(s_ref) for s_ref in (s0_ref, s1_ref, s2_ref)]
    top = jnp.maximum(jnp.maximum(stats[0], stats[1]), stats[2])
    e = [jnp.exp(s - top) for s in stats]
    den = [pltpu.roll(s, LANES - HEADS, axis=1) for s in stats]
    is_head = lax.broadcasted_iota(jnp.int32, top.shape, 1) < HEADS
    total = jnp.where(is_head, e[0] * den[0] + e[1] * den[1] + e[2] * den[2], 1.0)
    row = lax.broadcasted_iota(jnp.int32, (LANES, D_MODEL), 0)
    col = lax.broadcasted_iota(jnp.int32, (LANES, D_MODEL), 1)
    spread = jnp.where((col >= row * HEAD_DIM) & (col < (row + 1) * HEAD_DIM), 1.0, 0.0).astype(BF16)
    mix = None
    for e_g, o_ref in zip(e, (o0_ref, o1_ref, o2_ref)):
        w = jnp.dot(jnp.where(is_head, e_g / total, 0.0).astype(BF16), spread, preferred_element_type=F32)
        o = jnp.concatenate([o_ref[h].astype(F32) for h in range(HEADS)], axis=1)
        mix = w * o if mix is None else mix + w * o
    merged = mix.astype(BF16)
    x = x_ref[...] + jnp.dot(merged, wo_ref[...], preferred_element_type=F32)
    out_ref[...] = _swiglu_residual(x, g_ref, win_ref, wout_ref, h_ref)


def _merge_ffn(x, outs, stats, w_o, g, w_in, w_out, which, batch, seq):
    tm = MERGE_TOKENS
    tiles = seq // tm
    o_spec = pl.BlockSpec((None, HEADS, tm, HEAD_DIM), lambda i: (i // tiles, 0, i % tiles, 0))
    s_specs = [pl.BlockSpec((None, s.shape[1], tm, LANES), lambda i: (i // tiles, 0, i % tiles, 0)) for s in stats]
    tile = pl.BlockSpec((tm, D_MODEL), lambda i: (i, 0))
    return pl.pallas_call(
        _merge_ffn_kernel,
        grid=(batch * tiles,),
        in_specs=[tile, o_spec, o_spec, o_spec, *s_specs, _resident((D_MODEL, D_MODEL)),
                  _resident((1, D_MODEL)), _resident_at(which, (D_MODEL, 2 * D_FF)),
                  _resident_at(which, (D_FF, D_MODEL))],
        out_specs=tile,
        out_shape=jax.ShapeDtypeStruct(x.shape, F32),
        scratch_shapes=[pltpu.VMEM((tm, D_MODEL), BF16)],
        compiler_params=_params("parallel"),
        name="attn_merge_ffn",
    )(x, *outs, *stats, w_o.astype(BF16), g.reshape(1, D_MODEL), w_in, w_out)


def _attention_groups(x, g, w_qkv, q_norm, k_norm, batch, seq):
    w = w_qkv.astype(BF16)
    outs, stats = [], []
    for grp, ((_, dilation), (blocks, heads, unroll)) in enumerate(zip(ATTN_GROUPS, ATTN_TILING)):
        gains = jnp.stack([jnp.tile(q_norm[grp] * HEAD_DIM ** -0.5, HEADS), jnp.tile(k_norm[grp], HEADS)])
        qkv = _qkv(x, g, w, grp, gains, dilation, batch, seq)
        o, s = _attention(qkv, dilation, blocks, heads, unroll)
        outs.append(o)
        stats.append(s)
    return outs, stats


def kernel(x, norm_g, ffn_w_in, ffn_w_out, conv_w_pw1, conv_b_pw1, conv_w_dw, conv_b_dw, conv_norm_g,
           conv_w_pw2, conv_b_pw2, attn_w_qkv, attn_q_norm, attn_k_norm, attn_w_o):
    batch, seq, _ = x.shape
    xt = x.reshape(batch * seq, D_MODEL)
    w_in = ffn_w_in.astype(BF16)
    w_out = ffn_w_out.astype(BF16)
    for layer in range(norm_g.shape[0]):
        xt = _ffn(xt, norm_g[layer, 0], w_in, w_out, (layer, 0))
        idx = layer // 2
        if layer % 2 == 0:
            xt = _conv(xt, norm_g[layer, 1], conv_w_pw1[idx], conv_b_pw1[idx], conv_w_dw[idx], conv_b_dw[idx],
                       conv_norm_g[idx], conv_w_pw2[idx], conv_b_pw2[idx], batch, seq)
            xt = _ffn(xt, norm_g[layer, 2], w_in, w_out, (layer, 1))
        else:
            outs, stats = _attention_groups(xt, norm_g[layer, 1], attn_w_qkv[idx], attn_q_norm[idx],
                                            attn_k_norm[idx], batch, seq)
            xt = _merge_ffn(xt, outs, stats, attn_w_o[idx], norm_g[layer, 2], w_in, w_out, (layer, 1),
                            batch, seq)
    return xt.reshape(batch, seq, D_MODEL)
```

```python
import functools

import jax
import jax.numpy as jnp
from jax import lax
from jax.experimental import pallas as pl
from jax.experimental.pallas import tpu as pltpu

D_MODEL = 1024
D_FF = 2816
CONV_WIDTH = 31
ATTN_GROUPS = ((128, 1), (512, 4), (2048, 16))
HEADS = 8
HEAD_DIM = 128
ATTN_BLOCK = 128
NORM_EPS = 1e-6
MASKED = -1e30
LANES = 128
CONV_HALO = 32

F32 = jnp.float32
BF16 = jnp.bfloat16

FFN_TOKENS = 1024
FFN_CHUNK = 256
CONV_TOKENS = 512
CONV_ROWS = 64
QKV_TOKENS = 1024
QKV_STRIDE = 4
MERGE_TOKENS = 512
ATTN_TILING = ((8, 8, 1), (4, 8, 1), (1, 4, 8))
VMEM_LIMIT = 56 * 1024 * 1024


def _resident(shape):
    return pl.BlockSpec(shape, lambda *_: (0,) * len(shape), pipeline_mode=pl.Buffered(1))


def _resident_at(lead, shape):
    return pl.BlockSpec((None,) * len(lead) + shape, lambda *_: lead + (0,) * len(shape),
                        pipeline_mode=pl.Buffered(1))


def _params(*semantics):
    return pltpu.CompilerParams(dimension_semantics=semantics, vmem_limit_bytes=VMEM_LIMIT)


def _rmsnorm(x, g):
    return x * lax.rsqrt(jnp.mean(x * x, axis=-1, keepdims=True) + NORM_EPS) * g


def _sigmoid(x):
    return 1.0 / (1.0 + jnp.exp(-x))


def _swiglu_residual(x, g_ref, win_ref, wout_ref, h_ref):
    h_ref[...] = _rmsnorm(x, g_ref[...]).astype(BF16)
    acc = jnp.zeros(x.shape, F32)
    for c in range(D_FF // FFN_CHUNK):
        lo = c * FFN_CHUNK
        h = h_ref[...]
        gate = jnp.dot(h, win_ref[:, lo:lo + FFN_CHUNK], preferred_element_type=F32)
        up = jnp.dot(h, win_ref[:, D_FF + lo:D_FF + lo + FFN_CHUNK], preferred_element_type=F32)
        a = (gate * _sigmoid(gate) * up).astype(BF16)
        acc = acc + jnp.dot(a, wout_ref[lo:lo + FFN_CHUNK, :], preferred_element_type=F32)
    return x + 0.5 * acc


def _ffn_kernel(x_ref, g_ref, win_ref, wout_ref, o_ref, h_ref):
    o_ref[...] = _swiglu_residual(x_ref[...], g_ref, win_ref, wout_ref, h_ref)


def _ffn(x, g, w_in, w_out, which):
    tokens = x.shape[0]
    return pl.pallas_call(
        _ffn_kernel,
        grid=(tokens // FFN_TOKENS,),
        in_specs=[
            pl.BlockSpec((FFN_TOKENS, D_MODEL), lambda i: (i, 0)),
            _resident((1, D_MODEL)),
            _resident_at(which, (D_MODEL, 2 * D_FF)),
            _resident_at(which, (D_FF, D_MODEL)),
        ],
        out_specs=pl.BlockSpec((FFN_TOKENS, D_MODEL), lambda i: (i, 0)),
        out_shape=jax.ShapeDtypeStruct(x.shape, F32),
        scratch_shapes=[pltpu.VMEM((FFN_TOKENS, D_MODEL), BF16)],
        compiler_params=_params("parallel"),
        name="ffn",
    )(x, g.reshape(1, D_MODEL), w_in, w_out)


def _conv_kernel(x_ref, g_ref, w1_ref, b1_ref, wdw_ref, bdw_ref, gc_ref, w2_ref, b2_ref, o_ref,
                 u_ref, y_ref):
    ts = x_ref.shape[0]
    slabs = D_MODEL // LANES

    @pl.when(pl.program_id(1) == 0)
    def _():
        u_ref[:, 0:CONV_HALO, :] = jnp.zeros((slabs, CONV_HALO, LANES), F32)

    x = x_ref[...]
    h = _rmsnorm(x, g_ref[...]).astype(BF16)
    ag = jnp.dot(h, w1_ref[...], preferred_element_type=F32) + b1_ref[...]
    u = ag[:, :D_MODEL] * _sigmoid(ag[:, D_MODEL:])
    for cb in range(slabs):
        u_ref[cb, CONV_HALO:CONV_HALO + ts, :] = u[:, cb * LANES:(cb + 1) * LANES]

    first = CONV_HALO - (CONV_WIDTH - 1)
    for cb in range(slabs):
        cols = pl.ds(cb * LANES, LANES)
        w = wdw_ref[:, cols]
        b = bdw_ref[:, cols]
        for parity in range(2):
            for rc in range(ts // (2 * CONV_ROWS)):
                base = parity + 2 * rc * CONV_ROWS
                acc = jnp.broadcast_to(b, (CONV_ROWS, LANES))
                for k in range(CONV_WIDTH):
                    acc = acc + w[k:k + 1, :] * u_ref[cb, pl.ds(first + k + base, CONV_ROWS, stride=2), :]
                y_ref[cb, pl.ds(base, CONV_ROWS, stride=2), :] = acc

    for cb in range(slabs):
        u_ref[cb, 0:CONV_HALO, :] = u_ref[cb, ts:ts + CONV_HALO, :]

    y = _rmsnorm(jnp.concatenate([y_ref[cb] for cb in range(slabs)], axis=1), gc_ref[...])
    y = (y * _sigmoid(y)).astype(BF16)
    o_ref[...] = x + jnp.dot(y, w2_ref[...], preferred_element_type=F32) + b2_ref[...]


def _conv(x, g, w_pw1, b_pw1, w_dw, b_dw, g_conv, w_pw2, b_pw2, batch, seq):
    ts = CONV_TOKENS
    x3 = x.reshape(batch, seq, D_MODEL)
    tile = pl.BlockSpec((None, ts, D_MODEL), lambda b, s: (b, s, 0))
    out = pl.pallas_call(
        _conv_kernel,
        grid=(batch, seq // ts),
        in_specs=[
            tile,
            _resident((1, D_MODEL)),
            _resident((D_MODEL, 2 * D_MODEL)),
            _resident((1, 2 * D_MODEL)),
            _resident((CONV_WIDTH, D_MODEL)),
            _resident((1, D_MODEL)),
            _resident((1, D_MODEL)),
            _resident((D_MODEL, D_MODEL)),
            _resident((1, D_MODEL)),
        ],
        out_specs=tile,
        out_shape=jax.ShapeDtypeStruct(x3.shape, F32),
        scratch_shapes=[pltpu.VMEM((D_MODEL // LANES, CONV_HALO + ts, LANES), F32),
                        pltpu.VMEM((D_MODEL // LANES, ts, LANES), F32)],
        compiler_params=_params("parallel", "arbitrary"),
        name="conv",
    )(x3, g.reshape(1, D_MODEL), w_pw1.astype(BF16), b_pw1.reshape(1, -1), w_dw, b_dw.reshape(1, -1),
      g_conv.reshape(1, -1), w_pw2.astype(BF16), b_pw2.reshape(1, -1))
    return out.reshape(x.shape)


def _qkv_kernel(x_ref, g_ref, w_ref, gain_ref, o_ref, h_ref, *stage, dilation):
    tm = x_ref.shape[0]
    rows = tm // dilation
    h = _rmsnorm(x_ref[...], g_ref[...])
    if dilation == 1:
        h_ref[...] = h.astype(BF16)
    else:
        slab = stage[0]
        for cb in range(D_MODEL // LANES):
            slab[cb] = h[:, cb * LANES:(cb + 1) * LANES]
        if dilation > QKV_STRIDE:
            rest = dilation // QKV_STRIDE
            coarse = tm // QKV_STRIDE
            for cb in range(D_MODEL // LANES):
                for r0 in range(QKV_STRIDE):
                    stage[1][cb, r0 * coarse:(r0 + 1) * coarse, :] = slab[cb, pl.ds(r0, coarse, stride=QKV_STRIDE), :]
            for cb in range(D_MODEL // LANES):
                for r0 in range(QKV_STRIDE):
                    for r1 in range(rest):
                        r = r0 + QKV_STRIDE * r1
                        h_ref[r * rows:(r + 1) * rows, cb * LANES:(cb + 1) * LANES] = (
                            stage[1][cb, pl.ds(r0 * coarse + r1, rows, stride=rest), :].astype(BF16))
        else:
            for cb in range(D_MODEL // LANES):
                for r in range(dilation):
                    h_ref[r * rows:(r + 1) * rows, cb * LANES:(cb + 1) * LANES] = (
                        slab[cb, pl.ds(r, rows, stride=dilation), :].astype(BF16))

    pair = 2 * HEAD_DIM
    for j in range(3):
        for c in range(D_MODEL // pair):
            lo = j * D_MODEL + c * pair
            acc = jnp.dot(h_ref[...], w_ref[:, lo:lo + pair], preferred_element_type=F32)
            for sub in range(2):
                head = 2 * c + sub
                blk = acc[:, sub * HEAD_DIM:(sub + 1) * HEAD_DIM]
                if j < 2:
                    blk = _rmsnorm(blk, gain_ref[j:j + 1, head * HEAD_DIM:(head + 1) * HEAD_DIM])
                val = blk.astype(BF16)
                for r in range(dilation):
                    o_ref[j, head, r] = val[r * rows:(r + 1) * rows]


def _qkv(x, g, w, grp, gains, dilation, batch, seq):
    tm = QKV_TOKENS
    tiles = seq // tm
    rows = tm // dilation
    scratch = [pltpu.VMEM((tm, D_MODEL), BF16)]
    if dilation > 1:
        scratch.append(pltpu.VMEM((D_MODEL // LANES, tm, LANES), F32))
    if dilation > QKV_STRIDE:
        scratch.append(pltpu.VMEM((D_MODEL // LANES, tm, LANES), F32))
    return pl.pallas_call(
        functools.partial(_qkv_kernel, dilation=dilation),
        grid=(batch * tiles,),
        in_specs=[
            pl.BlockSpec((tm, D_MODEL), lambda i: (i, 0)),
            _resident((1, D_MODEL)),
            pl.BlockSpec((D_MODEL, 3 * D_MODEL), lambda i: (0, grp), pipeline_mode=pl.Buffered(1)),
            _resident((2, D_MODEL)),
        ],
        out_specs=pl.BlockSpec((None, 3, HEADS, dilation, rows, HEAD_DIM),
                               lambda i: (i // tiles, 0, 0, 0, i % tiles, 0)),
        out_shape=jax.ShapeDtypeStruct((batch, 3, HEADS, dilation, seq // dilation, HEAD_DIM), BF16),
        scratch_shapes=scratch,
        compiler_params=_params("parallel"),
        name=f"qkv_d{dilation}",
    )(x, g.reshape(1, D_MODEL), w, gains)


def _attn_kernel(q_ref, kc_ref, vc_ref, kp_ref, vp_ref, o_ref, stat_ref, *stage, dilation, blocks, heads, unroll):
    qb = ATTN_BLOCK
    o_f32 = stage[0] if dilation > 1 else None
    row = lax.broadcasted_iota(jnp.int32, (qb, 2 * qb), 0)
    col = lax.broadcasted_iota(jnp.int32, (qb, 2 * qb), 1)
    bias = jnp.where((col >= row) & (col <= row + qb), 0.0, MASKED).astype(F32)
    bias_first = jnp.where((pl.program_id(1) == 0) & (col < qb), MASKED, bias)
    lane = lax.broadcasted_iota(jnp.int32, (qb, LANES), 1)
    head0 = pl.program_id(2) * heads
    ones = jnp.ones((2 * qb, HEAD_DIM), BF16)
    nt = (((1,), (1,)), ((), ()))

    def stream(r):
        for b in range(blocks):
            cur = pl.ds(b * qb, qb)
            if dilation == 1:
                out_rows = cur
            else:
                out_rows = pl.ds(r + b * qb * dilation, qb, stride=dilation)
            stats = jnp.zeros((qb, LANES), F32)
            for h in range(heads):
                q = q_ref[h, r, cur, :]
                if b == 0:
                    k2 = jnp.concatenate([kp_ref[h, r], kc_ref[h, r, cur, :]], axis=0)
                    v2 = jnp.concatenate([vp_ref[h, r], vc_ref[h, r, cur, :]], axis=0)
                    bb = bias_first
                else:
                    both = pl.ds((b - 1) * qb, 2 * qb)
                    k2, v2, bb = kc_ref[h, r, both, :], vc_ref[h, r, both, :], bias
                s = lax.dot_general(q, k2, nt, preferred_element_type=F32) + bb
                m = jnp.max(s, axis=-1, keepdims=True)
                p = jnp.exp(s - m).astype(BF16)
                oa = jnp.dot(p, jnp.concatenate([v2, ones], axis=1), preferred_element_type=F32)
                if dilation == 1:
                    o_ref[h, out_rows, :] = oa[:, :HEAD_DIM].astype(BF16)
                else:
                    o_f32[h, out_rows, :] = oa[:, :HEAD_DIM]
                stats = jnp.where(lane == head0 + h, m, stats)
                stats = jnp.where(lane == HEADS + head0 + h, oa[:, HEAD_DIM:], stats)
            stat_ref[out_rows, :] = stats

    def body(it, carry):
        for sub in range(unroll):
            stream(it * unroll + sub)
        return carry

    if dilation == unroll:
        body(0, 0)
    else:
        lax.fori_loop(0, dilation // unroll, body, 0)
    if dilation > 1:
        for h in range(heads):
            o_ref[h] = o_f32[h].astype(BF16)


def _attention(qkv, dilation, blocks, heads, unroll):
    batch, _, _, _, length, _ = qkv.shape
    rb = blocks * ATTN_BLOCK
    seq = length * dilation
    steps = length // rb

    def cur(which):
        return pl.BlockSpec((None, None, heads, dilation, rb, HEAD_DIM),
                            lambda b, n, hs: (b, which, hs, 0, n, 0))

    def prev(which):
        return pl.BlockSpec((None, None, heads, dilation, ATTN_BLOCK, HEAD_DIM),
                            lambda b, n, hs: (b, which, hs, 0, jnp.maximum(n * blocks - 1, 0), 0))

    return pl.pallas_call(
        functools.partial(_attn_kernel, dilation=dilation, blocks=blocks, heads=heads, unroll=unroll),
        grid=(batch, steps, HEADS // heads),
        in_specs=[cur(0), cur(1), cur(2), prev(1), prev(2)],
        out_specs=[
            pl.BlockSpec((None, heads, rb * dilation, HEAD_DIM), lambda b, n, hs: (b, hs, n, 0)),
            pl.BlockSpec((None, None, rb * dilation, LANES), lambda b, n, hs: (b, hs, n, 0)),
        ],
        out_shape=[
            jax.ShapeDtypeStruct((batch, HEADS, seq, HEAD_DIM), BF16),
            jax.ShapeDtypeStruct((batch, HEADS // heads, seq, LANES), F32),
        ],
        scratch_shapes=[pltpu.VMEM((heads, rb * dilation, HEAD_DIM), F32)] if dilation > 1 else [],
        compiler_params=_params("parallel", "parallel", "parallel"),
        name=f"attn_d{dilation}",
    )(qkv, qkv, qkv, qkv, qkv)


def _merge_ffn_kernel(x_ref, o0_ref, o1_ref, o2_ref, s0_ref, s1_ref, s2_ref, wo_ref, g_ref, win_ref, wout_ref,
                      out_ref, h_ref):
    def per_head(s_ref):
        s = s_ref[0]
        for t in range(1, s_ref.shape[0]):
            s = s + s_ref[t]
        return s

    stats = [per_head(s_ref) for s_ref in (s0_ref, s1_ref, s2_ref)]
    top = jnp.maximum(jnp.maximum(stats[0], stats[1]), stats[2])
    e = [jnp.exp(s - top) for s in stats]
    den = [pltpu.roll(s, LANES - HEADS, axis=1) for s in stats]
    is_head = lax.broadcasted_iota(jnp.int32, top.shape, 1) < HEADS
    total = jnp.where(is_head, e[0] * den[0] + e[1] * den[1] + e[2] * den[2], 1.0)
    row = lax.broadcasted_iota(jnp.int32, (LANES, D_MODEL), 0)
    col = lax.broadcasted_iota(jnp.int32, (LANES, D_MODEL), 1)
    spread = jnp.where((col >= row * HEAD_DIM) & (col < (row + 1) * HEAD_DIM), 1.0, 0.0).astype(BF16)
    mix = None
    for e_g, o_ref in zip(e, (o0_ref, o1_ref, o2_ref)):
        w = jnp.dot(jnp.where(is_head, e_g / total, 0.0).astype(BF16), spread, preferred_element_type=F32)
        o = jnp.concatenate([o_ref[h].astype(F32) for h in range(HEADS)], axis=1)
        mix = w * o if mix is None else mix + w * o
    merged = mix.astype(BF16)
    x = x_ref[...] + jnp.dot(merged, wo_ref[...], preferred_element_type=F32)
    out_ref[...] = _swiglu_residual(x, g_ref, win_ref, wout_ref, h_ref)


def _merge_ffn(x, outs, stats, w_o, g, w_in, w_out, which, batch, seq):
    tm = MERGE_TOKENS
    tiles = seq // tm
    o_spec = pl.BlockSpec((None, HEADS, tm, HEAD_DIM), lambda i: (i // tiles, 0, i % tiles, 0))
    s_specs = [pl.BlockSpec((None, s.shape[1], tm, LANES), lambda i: (i // tiles, 0, i % tiles, 0)) for s in stats]
    tile = pl.BlockSpec((tm, D_MODEL), lambda i: (i, 0))
    return pl.pallas_call(
        _merge_ffn_kernel,
        grid=(batch * tiles,),
        in_specs=[tile, o_spec, o_spec, o_spec, *s_specs, _resident((D_MODEL, D_MODEL)),
                  _resident((1, D_MODEL)), _resident_at(which, (D_MODEL, 2 * D_FF)),
                  _resident_at(which, (D_FF, D_MODEL))],
        out_specs=tile,
        out_shape=jax.ShapeDtypeStruct(x.shape, F32),
        scratch_shapes=[pltpu.VMEM((tm, D_MODEL), BF16)],
        compiler_params=_params("parallel"),
        name="attn_merge_ffn",
    )(x, *outs, *stats, w_o.astype(BF16), g.reshape(1, D_MODEL), w_in, w_out)


def _attention_groups(x, g, w_qkv, q_norm, k_norm, batch, seq):
    w = w_qkv.astype(BF16)
    outs, stats = [], []
    for grp, ((_, dilation), (blocks, heads, unroll)) in enumerate(zip(ATTN_GROUPS, ATTN_TILING)):
        gains = jnp.stack([jnp.tile(q_norm[grp] * HEAD_DIM ** -0.5, HEADS), jnp.tile(k_norm[grp], HEADS)])
        qkv = _qkv(x, g, w, grp, gains, dilation, batch, seq)
        o, s = _attention(qkv, dilation, blocks, heads, unroll)
        outs.append(o)
        stats.append(s)
    return outs, stats


def kernel(x, norm_g, ffn_w_in, ffn_w_out, conv_w_pw1, conv_b_pw1, conv_w_dw, conv_b_dw, conv_norm_g,
           conv_w_pw2, conv_b_pw2, attn_w_qkv, attn_q_norm, attn_k_norm, attn_w_o):
    batch, seq, _ = x.shape
    xt = x.reshape(batch * seq, D_MODEL)
    w_in = ffn_w_in.astype(BF16)
    w_out = ffn_w_out.astype(BF16)
    for layer in range(norm_g.shape[0]):
        xt = _ffn(xt, norm_g[layer, 0], w_in, w_out, (layer, 0))
        idx = layer // 2
        if layer % 2 == 0:
            xt = _conv(xt, norm_g[layer, 1], conv_w_pw1[idx], conv_b_pw1[idx], conv_w_dw[idx], conv_b_dw[idx],
                       conv_norm_g[idx], conv_w_pw2[idx], conv_b_pw2[idx], batch, seq)
            xt = _ffn(xt, norm_g[layer, 2], w_in, w_out, (layer, 1))
        else:
            outs, stats = _attention_groups(xt, norm_g[layer, 1], attn_w_qkv[idx], attn_q_norm[idx],
                                            attn_k_norm[idx], batch, seq)
            xt = _merge_ffn(xt, outs, stats, attn_w_o[idx], norm_g[layer, 2], w_in, w_out, (layer, 1),
                            batch, seq)
    return xt.reshape(batch, seq, D_MODEL)
```

```python
import functools

import jax
import jax.numpy as jnp
from jax import lax
from jax.experimental import pallas as pl
from jax.experimental.pallas import tpu as pltpu

D_MODEL = 1024
D_FF = 2816
CONV_WIDTH = 31
ATTN_GROUPS = ((128, 1), (512, 4), (2048, 16))
HEADS = 8
HEAD_DIM = 128
ATTN_BLOCK = 128
NORM_EPS = 1e-6
MASKED = -1e30
LANES = 128
CONV_HALO = 32

F32 = jnp.float32
BF16 = jnp.bfloat16

FFN_TOKENS = 1024
FFN_CHUNK = 256
CONV_TOKENS = 1024
CONV_ROWS = 64
QKV_TOKENS = 1024
QKV_STRIDE = 4
MERGE_TOKENS = 512
ATTN_TILING = ((16, 8, 1), (4, 8, 1), (1, 4, 8))
VMEM_LIMIT = 56 * 1024 * 1024


def _resident(shape):
    return pl.BlockSpec(shape, lambda *_: (0,) * len(shape), pipeline_mode=pl.Buffered(1))


def _resident_at(lead, shape):
    return pl.BlockSpec((None,) * len(lead) + shape, lambda *_: lead + (0,) * len(shape),
                        pipeline_mode=pl.Buffered(1))


def _params(*semantics):
    return pltpu.CompilerParams(dimension_semantics=semantics, vmem_limit_bytes=VMEM_LIMIT)


def _rmsnorm(x, g):
    return x * lax.rsqrt(jnp.mean(x * x, axis=-1, keepdims=True) + NORM_EPS) * g


def _sigmoid(x):
    return 1.0 / (1.0 + jnp.exp(-x))


def _swiglu_residual(x, g_ref, win_ref, wout_ref, h_ref):
    h_ref[...] = _rmsnorm(x, g_ref[...]).astype(BF16)
    acc = jnp.zeros(x.shape, F32)
    for c in range(D_FF // FFN_CHUNK):
        lo = c * FFN_CHUNK
        h = h_ref[...]
        gate = jnp.dot(h, win_ref[:, lo:lo + FFN_CHUNK], preferred_element_type=F32)
        up = jnp.dot(h, win_ref[:, D_FF + lo:D_FF + lo + FFN_CHUNK], preferred_element_type=F32)
        a = (gate * _sigmoid(gate) * up).astype(BF16)
        acc = acc + jnp.dot(a, wout_ref[lo:lo + FFN_CHUNK, :], preferred_element_type=F32)
    return x + 0.5 * acc


def _ffn_kernel(x_ref, g_ref, win_ref, wout_ref, o_ref, h_ref):
    o_ref[...] = _swiglu_residual(x_ref[...], g_ref, win_ref, wout_ref, h_ref)


def _ffn(x, g, w_in, w_out, which):
    tokens = x.shape[0]
    return pl.pallas_call(
        _ffn_kernel,
        grid=(tokens // FFN_TOKENS,),
        in_specs=[
            pl.BlockSpec((FFN_TOKENS, D_MODEL), lambda i: (i, 0)),
            _resident((1, D_MODEL)),
            _resident_at(which, (D_MODEL, 2 * D_FF)),
            _resident_at(which, (D_FF, D_MODEL)),
        ],
        out_specs=pl.BlockSpec((FFN_TOKENS, D_MODEL), lambda i: (i, 0)),
        out_shape=jax.ShapeDtypeStruct(x.shape, F32),
        scratch_shapes=[pltpu.VMEM((FFN_TOKENS, D_MODEL), BF16)],
        compiler_params=_params("parallel"),
        name="ffn",
    )(x, g.reshape(1, D_MODEL), w_in, w_out)


def _conv_kernel(x_ref, g_ref, w1_ref, b1_ref, wdw_ref, bdw_ref, gc_ref, w2_ref, b2_ref, o_ref,
                 u_ref, y_ref):
    ts = x_ref.shape[0]
    slabs = D_MODEL // LANES

    @pl.when(pl.program_id(1) == 0)
    def _():
        u_ref[:, 0:CONV_HALO, :] = jnp.zeros((slabs, CONV_HALO, LANES), F32)

    x = x_ref[...]
    h = _rmsnorm(x, g_ref[...]).astype(BF16)
    ag = jnp.dot(h, w1_ref[...], preferred_element_type=F32) + b1_ref[...]
    u = ag[:, :D_MODEL] * _sigmoid(ag[:, D_MODEL:])
    for cb in range(slabs):
        u_ref[cb, CONV_HALO:CONV_HALO + ts, :] = u[:, cb * LANES:(cb + 1) * LANES]

    first = CONV_HALO - (CONV_WIDTH - 1)
    for cb in range(slabs):
        cols = pl.ds(cb * LANES, LANES)
        w = wdw_ref[:, cols]
        b = bdw_ref[:, cols]
        for parity in range(2):
            for rc in range(ts // (2 * CONV_ROWS)):
                base = parity + 2 * rc * CONV_ROWS
                acc = jnp.broadcast_to(b, (CONV_ROWS, LANES))
                for k in range(CONV_WIDTH):
                    acc = acc + w[k:k + 1, :] * u_ref[cb, pl.ds(first + k + base, CONV_ROWS, stride=2), :]
                y_ref[cb, pl.ds(base, CONV_ROWS, stride=2), :] = acc

    for cb in range(slabs):
        u_ref[cb, 0:CONV_HALO, :] = u_ref[cb, ts:ts + CONV_HALO, :]

    y = _rmsnorm(jnp.concatenate([y_ref[cb] for cb in range(slabs)], axis=1), gc_ref[...])
    y = (y * _sigmoid(y)).astype(BF16)
    o_ref[...] = x + jnp.dot(y, w2_ref[...], preferred_element_type=F32) + b2_ref[...]


def _conv(x, g, w_pw1, b_pw1, w_dw, b_dw, g_conv, w_pw2, b_pw2, batch, seq):
    ts = CONV_TOKENS
    x3 = x.reshape(batch, seq, D_MODEL)
    tile = pl.BlockSpec((None, ts, D_MODEL), lambda b, s: (b, s, 0))
    out = pl.pallas_call(
        _conv_kernel,
        grid=(batch, seq // ts),
        in_specs=[
            tile,
            _resident((1, D_MODEL)),
            _resident((D_MODEL, 2 * D_MODEL)),
            _resident((1, 2 * D_MODEL)),
            _resident((CONV_WIDTH, D_MODEL)),
            _resident((1, D_MODEL)),
            _resident((1, D_MODEL)),
            _resident((D_MODEL, D_MODEL)),
            _resident((1, D_MODEL)),
        ],
        out_specs=tile,
        out_shape=jax.ShapeDtypeStruct(x3.shape, F32),
        scratch_shapes=[pltpu.VMEM((D_MODEL // LANES, CONV_HALO + ts, LANES), F32),
                        pltpu.VMEM((D_MODEL // LANES, ts, LANES), F32)],
        compiler_params=_params("parallel", "arbitrary"),
        name="conv",
    )(x3, g.reshape(1, D_MODEL), w_pw1.astype(BF16), b_pw1.reshape(1, -1), w_dw, b_dw.reshape(1, -1),
      g_conv.reshape(1, -1), w_pw2.astype(BF16), b_pw2.reshape(1, -1))
    return out.reshape(x.shape)


def _qkv_kernel(x_ref, g_ref, w_ref, gain_ref, o_ref, h_ref, *stage, dilation):
    tm = x_ref.shape[0]
    rows = tm // dilation
    h = _rmsnorm(x_ref[...], g_ref[...])
    if dilation == 1:
        h_ref[...] = h.astype(BF16)
    else:
        slab = stage[0]
        for cb in range(D_MODEL // LANES):
            slab[cb] = h[:, cb * LANES:(cb + 1) * LANES]
        if dilation > QKV_STRIDE:
            rest = dilation // QKV_STRIDE
            coarse = tm // QKV_STRIDE
            for cb in range(D_MODEL // LANES):
                for r0 in range(QKV_STRIDE):
                    stage[1][cb, r0 * coarse:(r0 + 1) * coarse, :] = slab[cb, pl.ds(r0, coarse, stride=QKV_STRIDE), :]
            for cb in range(D_MODEL // LANES):
                for r0 in range(QKV_STRIDE):
                    for r1 in range(rest):
                        r = r0 + QKV_STRIDE * r1
                        h_ref[r * rows:(r + 1) * rows, cb * LANES:(cb + 1) * LANES] = (
                            stage[1][cb, pl.ds(r0 * coarse + r1, rows, stride=rest), :].astype(BF16))
        else:
            for cb in range(D_MODEL // LANES):
                for r in range(dilation):
                    h_ref[r * rows:(r + 1) * rows, cb * LANES:(cb + 1) * LANES] = (
                        slab[cb, pl.ds(r, rows, stride=dilation), :].astype(BF16))

    pair = 2 * HEAD_DIM
    for j in range(3):
        for c in range(D_MODEL // pair):
            lo = j * D_MODEL + c * pair
            acc = jnp.dot(h_ref[...], w_ref[:, lo:lo + pair], preferred_element_type=F32)
            for sub in range(2):
                head = 2 * c + sub
                blk = acc[:, sub * HEAD_DIM:(sub + 1) * HEAD_DIM]
                if j < 2:
                    blk = _rmsnorm(blk, gain_ref[j:j + 1, head * HEAD_DIM:(head + 1) * HEAD_DIM])
                val = blk.astype(BF16)
                for r in range(dilation):
                    o_ref[j, head, r] = val[r * rows:(r + 1) * rows]


def _qkv(x, g, w, grp, gains, dilation, batch, seq):
    tm = QKV_TOKENS
    tiles = seq // tm
    rows = tm // dilation
    scratch = [pltpu.VMEM((tm, D_MODEL), BF16)]
    if dilation > 1:
        scratch.append(pltpu.VMEM((D_MODEL // LANES, tm, LANES), F32))
    if dilation > QKV_STRIDE:
        scratch.append(pltpu.VMEM((D_MODEL // LANES, tm, LANES), F32))
    return pl.pallas_call(
        functools.partial(_qkv_kernel, dilation=dilation),
        grid=(batch * tiles,),
        in_specs=[
            pl.BlockSpec((tm, D_MODEL), lambda i: (i, 0)),
            _resident((1, D_MODEL)),
            pl.BlockSpec((D_MODEL, 3 * D_MODEL), lambda i: (0, grp), pipeline_mode=pl.Buffered(1)),
            _resident((2, D_MODEL)),
        ],
        out_specs=pl.BlockSpec((None, 3, HEADS, dilation, rows, HEAD_DIM),
                               lambda i: (i // tiles, 0, 0, 0, i % tiles, 0)),
        out_shape=jax.ShapeDtypeStruct((batch, 3, HEADS, dilation, seq // dilation, HEAD_DIM), BF16),
        scratch_shapes=scratch,
        compiler_params=_params("parallel"),
        name=f"qkv_d{dilation}",
    )(x, g.reshape(1, D_MODEL), w, gains)


def _attn_kernel(q_ref, kc_ref, vc_ref, kp_ref, vp_ref, o_ref, stat_ref, *stage, dilation, blocks, heads, unroll):
    qb = ATTN_BLOCK
    o_f32 = stage[0] if dilation > 1 else None
    row = lax.broadcasted_iota(jnp.int32, (qb, 2 * qb), 0)
    col = lax.broadcasted_iota(jnp.int32, (qb, 2 * qb), 1)
    bias = jnp.where((col >= row) & (col <= row + qb), 0.0, MASKED).astype(F32)
    bias_first = jnp.where((pl.program_id(1) == 0) & (col < qb), MASKED, bias)
    lane = lax.broadcasted_iota(jnp.int32, (qb, LANES), 1)
    head0 = pl.program_id(2) * heads
    ones = jnp.ones((2 * qb, HEAD_DIM), BF16)
    nt = (((1,), (1,)), ((), ()))

    def stream(r):
        for b in range(blocks):
            cur = pl.ds(b * qb, qb)
            if dilation == 1:
                out_rows = cur
            else:
                out_rows = pl.ds(r + b * qb * dilation, qb, stride=dilation)
            stats = jnp.zeros((qb, LANES), F32)
            for h in range(heads):
                q = q_ref[h, r, cur, :]
                if b == 0:
                    k2 = jnp.concatenate([kp_ref[h, r], kc_ref[h, r, cur, :]], axis=0)
                    v2 = jnp.concatenate([vp_ref[h, r], vc_ref[h, r, cur, :]], axis=0)
                    bb = bias_first
                else:
                    both = pl.ds((b - 1) * qb, 2 * qb)
                    k2, v2, bb = kc_ref[h, r, both, :], vc_ref[h, r, both, :], bias
                s = lax.dot_general(q, k2, nt, preferred_element_type=F32) + bb
                m = jnp.max(s, axis=-1, keepdims=True)
                p = jnp.exp(s - m).astype(BF16)
                oa = jnp.dot(p, jnp.concatenate([v2, ones], axis=1), preferred_element_type=F32)
                if dilation == 1:
                    o_ref[h, out_rows, :] = oa[:, :HEAD_DIM].astype(BF16)
                else:
                    o_f32[h, out_rows, :] = oa[:, :HEAD_DIM]
                stats = jnp.where(lane == head0 + h, m, stats)
                stats = jnp.where(lane == HEADS + head0 + h, oa[:, HEAD_DIM:], stats)
            stat_ref[out_rows, :] = stats

    def body(it, carry):
        for sub in range(unroll):
            stream(it * unroll + sub)
        return carry

    if dilation == unroll:
        body(0, 0)
    else:
        lax.fori_loop(0, dilation // unroll, body, 0)
    if dilation > 1:
        for h in range(heads):
            o_ref[h] = o_f32[h].astype(BF16)


def _attention(qkv, dilation, blocks, heads, unroll):
    batch, _, _, _, length, _ = qkv.shape
    rb = blocks * ATTN_BLOCK
    seq = length * dilation
    steps = length // rb

    def cur(which):
        return pl.BlockSpec((None, None, heads, dilation, rb, HEAD_DIM),
                            lambda b, n, hs: (b, which, hs, 0, n, 0))

    def prev(which):
        return pl.BlockSpec((None, None, heads, dilation, ATTN_BLOCK, HEAD_DIM),
                            lambda b, n, hs: (b, which, hs, 0, jnp.maximum(n * blocks - 1, 0), 0))

    return pl.pallas_call(
        functools.partial(_attn_kernel, dilation=dilation, blocks=blocks, heads=heads, unroll=unroll),
        grid=(batch, steps, HEADS // heads),
        in_specs=[cur(0), cur(1), cur(2), prev(1), prev(2)],
        out_specs=[
            pl.BlockSpec((None, heads, rb * dilation, HEAD_DIM), lambda b, n, hs: (b, hs, n, 0)),
            pl.BlockSpec((None, None, rb * dilation, LANES), lambda b, n, hs: (b, hs, n, 0)),
        ],
        out_shape=[
            jax.ShapeDtypeStruct((batch, HEADS, seq, HEAD_DIM), BF16),
            jax.ShapeDtypeStruct((batch, HEADS // heads, seq, LANES), F32),
        ],
        scratch_shapes=[pltpu.VMEM((heads, rb * dilation, HEAD_DIM), F32)] if dilation > 1 else [],
        compiler_params=_params("parallel", "parallel", "parallel"),
        name=f"attn_d{dilation}",
    )(qkv, qkv, qkv, qkv, qkv)


def _merge_ffn_kernel(x_ref, o0_ref, o1_ref, o2_ref, s0_ref, s1_ref, s2_ref, wo_ref, g_ref, win_ref, wout_ref,
                      out_ref, h_ref):
    def per_head(s_ref):
        s = s_ref[0]
        for t in range(1, s_ref.shape[0]):
            s = s + s_ref[t]
        return s

    stats = [per_head(s_ref) for s_ref in (s0_ref, s1_ref, s2_ref)]
    top = jnp.maximum(jnp.maximum(stats[0], stats[1]), stats[2])
    e = [jnp.exp(s - top) for s in stats]
    den = [pltpu.roll(s, LANES - HEADS, axis=1) for s in stats]
    is_head = lax.broadcasted_iota(jnp.int32, top.shape, 1) < HEADS
    total = jnp.where(is_head, e[0] * den[0] + e[1] * den[1] + e[2] * den[2], 1.0)
    row = lax.broadcasted_iota(jnp.int32, (LANES, D_MODEL), 0)
    col = lax.broadcasted_iota(jnp.int32, (LANES, D_MODEL), 1)
    spread = jnp.where((col >= row * HEAD_DIM) & (col < (row + 1) * HEAD_DIM), 1.0, 0.0).astype(BF16)
    mix = None
    for e_g, o_ref in zip(e, (o0_ref, o1_ref, o2_ref)):
        w = jnp.dot(jnp.where(is_head, e_g / total, 0.0).astype(BF16), spread, preferred_element_type=F32)
        o = jnp.concatenate([o_ref[h].astype(F32) for h in range(HEADS)], axis=1)
        mix = w * o if mix is None else mix + w * o
    merged = mix.astype(BF16)
    x = x_ref[...] + jnp.dot(merged, wo_ref[...], preferred_element_type=F32)
    out_ref[...] = _swiglu_residual(x, g_ref, win_ref, wout_ref, h_ref)


def _merge_ffn(x, outs, stats, w_o, g, w_in, w_out, which, batch, seq):
    tm = MERGE_TOKENS
    tiles = seq // tm
    o_spec = pl.BlockSpec((None, HEADS, tm, HEAD_DIM), lambda i: (i // tiles, 0, i % tiles, 0))
    s_specs = [pl.BlockSpec((None, s.shape[1], tm, LANES), lambda i: (i // tiles, 0, i % tiles, 0)) for s in stats]
    tile = pl.BlockSpec((tm, D_MODEL), lambda i: (i, 0))
    return pl.pallas_call(
        _merge_ffn_kernel,
        grid=(batch * tiles,),
        in_specs=[tile, o_spec, o_spec, o_spec, *s_specs, _resident((D_MODEL, D_MODEL)),
                  _resident((1, D_MODEL)), _resident_at(which, (D_MODEL, 2 * D_FF)),
                  _resident_at(which, (D_FF, D_MODEL))],
        out_specs=tile,
        out_shape=jax.ShapeDtypeStruct(x.shape, F32),
        scratch_shapes=[pltpu.VMEM((tm, D_MODEL), BF16)],
        compiler_params=_params("parallel"),
        name="attn_merge_ffn",
    )(x, *outs, *stats, w_o.astype(BF16), g.reshape(1, D_MODEL), w_in, w_out)


def _attention_groups(x, g, w_qkv, q_norm, k_norm, batch, seq):
    w = w_qkv.astype(BF16)
    outs, stats = [], []
    for grp, ((_, dilation), (blocks, heads, unroll)) in enumerate(zip(ATTN_GROUPS, ATTN_TILING)):
        gains = jnp.stack([jnp.tile(q_norm[grp] * HEAD_DIM ** -0.5, HEADS), jnp.tile(k_norm[grp], HEADS)])
        qkv = _qkv(x, g, w, grp, gains, dilation, batch, seq)
        o, s = _attention(qkv, dilation, blocks, heads, unroll)
        outs.append(o)
        stats.append(s)
    return outs, stats


def kernel(x, norm_g, ffn_w_in, ffn_w_out, conv_w_pw1, conv_b_pw1, conv_w_dw, conv_b_dw, conv_norm_g,
           conv_w_pw2, conv_b_pw2, attn_w_qkv, attn_q_norm, attn_k_norm, attn_w_o):
    batch, seq, _ = x.shape
    xt = x.reshape(batch * seq, D_MODEL)
    w_in = ffn_w_in.astype(BF16)
    w_out = ffn_w_out.astype(BF16)
    for layer in range(norm_g.shape[0]):
        xt = _ffn(xt, norm_g[layer, 0], w_in, w_out, (layer, 0))
        idx = layer // 2
        if layer % 2 == 0:
            xt = _conv(xt, norm_g[layer, 1], conv_w_pw1[idx], conv_b_pw1[idx], conv_w_dw[idx], conv_b_dw[idx],
                       conv_norm_g[idx], conv_w_pw2[idx], conv_b_pw2[idx], batch, seq)
            xt = _ffn(xt, norm_g[layer, 2], w_in, w_out, (layer, 1))
        else:
            outs, stats = _attention_groups(xt, norm_g[layer, 1], attn_w_qkv[idx], attn_q_norm[idx],
                                            attn_k_norm[idx], batch, seq)
            xt = _merge_ffn(xt, outs, stats, attn_w_o[idx], norm_g[layer, 2], w_in, w_out, (layer, 1),
                            batch, seq)
    return xt.reshape(batch, seq, D_MODEL)
```
